```python
import math
import jax, jax.numpy as jnp
from jax import lax
import numpy as np

D_MODEL = 4096
BATCH = 2
SEQ = 4096
DEPTH = 1
DEC_BATCH = 32
DEC_SEQ = 1
PAST_LEN = 8192
PAGE_SIZE = 128

N_META = 16
ATTN_WIDTH = D_MODEL // 2
CONV_WIDTH = D_MODEL - ATTN_WIDTH
N_HEADS = 16
HEAD_DIM = ATTN_WIDTH // N_HEADS
QK_DIM = HEAD_DIM // 2
CONV_K = 31
N_EXPERTS = 32
TOP_K = 4
D_FF = D_MODEL
SWIGLU_LIMIT = 7.0
SWIGLU_ALPHA = 1.702
ROPE_THETA = 10000.0
Q_BLOCK = 128
MOE_BLOCK = 128
NORM_EPS = 1e-6
LN_EPS = 1e-5
D_IN = 3 * ATTN_WIDTH + 2 * CONV_WIDTH

kernel_name = "hymba_diffattn_conformer_moe_step"


def rms_norm(x, g, eps=NORM_EPS):
    xf = x.astype(jnp.float32)
    y = xf * lax.rsqrt(jnp.mean(xf * xf, axis=-1, keepdims=True) + eps)
    return (y * g.astype(jnp.float32)).astype(x.dtype)


def layer_norm(x, g, b, eps=LN_EPS):
    xf = x.astype(jnp.float32)
    mu = jnp.mean(xf, axis=-1, keepdims=True)
    var = jnp.mean(jnp.square(xf - mu), axis=-1, keepdims=True)
    y = (xf - mu) * lax.rsqrt(var + eps) * g.astype(jnp.float32) + b.astype(jnp.float32)
    return y.astype(x.dtype)


def rope(x, pos):
    d = x.shape[-1]
    inv = ROPE_THETA ** (-jnp.arange(0, d, 2, dtype=jnp.float32) / d)
    ang = pos.astype(jnp.float32)[:, None] * inv[None, :]
    cos = jnp.cos(ang)[:, None, :]
    sin = jnp.sin(ang)[:, None, :]
    xf = x.astype(jnp.float32)
    x1, x2 = xf[..., : d // 2], xf[..., d // 2:]
    return jnp.concatenate([x1 * cos - x2 * sin, x2 * cos + x1 * sin], axis=-1).astype(x.dtype)


def diff_lambda(lq1, lk1, lq2, lk2, lam_init):
    f = lambda a: a.astype(jnp.float32)
    return jnp.exp(jnp.sum(f(lq1) * f(lk1))) - jnp.exp(jnp.sum(f(lq2) * f(lk2))) + lam_init


def diff_combine(s, lam):
    p = jax.nn.softmax(s, axis=-1)
    p = p.reshape(p.shape[:-3] + (N_HEADS, 2) + p.shape[-2:])
    return p[..., 0, :, :] - lam * p[..., 1, :, :]


def diff_attn_prompt(q, k, v, lam):
    B, L = q.shape[0], q.shape[1]
    n_qb = -(-L // Q_BLOCK)
    Lp = n_qb * Q_BLOCK
    qp = jnp.pad(q, ((0, 0), (0, Lp - L), (0, 0), (0, 0)))
    qb = qp.reshape(B, n_qb, Q_BLOCK, 2 * N_HEADS, QK_DIM).transpose(1, 0, 2, 3, 4)
    k_pos = jnp.arange(L)
    scale = QK_DIM ** -0.5

    def block(args):
        qi, start = args
        s = jnp.einsum('bqhd,bkhd->bhqk', qi, k, preferred_element_type=jnp.float32) * scale
        q_pos = start + jnp.arange(Q_BLOCK)
        s = jnp.where(k_pos[None, :] <= q_pos[:, None], s, -jnp.inf)
        a = diff_combine(s, lam)
        return jnp.einsum('bhqk,bkhd->bqhd', a.astype(v.dtype), v)

    o = lax.map(block, (qb, jnp.arange(n_qb) * Q_BLOCK))
    o = o.transpose(1, 0, 2, 3, 4).reshape(B, Lp, N_HEADS, HEAD_DIM)
    return o[:, :L]


def diff_attn_sample(q, k_new, v_new, cache_k, cache_v, layer, page_table, lam):
    S = q.shape[1]
    causal = jnp.arange(S)[None, :] <= jnp.arange(S)[:, None]
    scale = QK_DIM ** -0.5

    def one(args):
        qi, ki, vi, pt = args
        kp = cache_k[layer, pt].reshape(-1, 2 * N_HEADS, QK_DIM)
        vp = cache_v[layer, pt].reshape(-1, N_HEADS, HEAD_DIM)
        n_past = kp.shape[0]
        s_past = jnp.einsum('qhd,khd->hqk', qi, kp, preferred_element_type=jnp.float32)
        s_new = jnp.einsum('qhd,khd->hqk', qi, ki, preferred_element_type=jnp.float32)
        s_new = jnp.where(causal, s_new, -jnp.inf)
        s = jnp.concatenate([s_past, s_new], axis=-1) * scale
        a = diff_combine(s, lam).astype(vi.dtype)
        return (jnp.einsum('hqk,khd->qhd', a[..., :n_past], vp)
                + jnp.einsum('hqk,khd->qhd', a[..., n_past:], vi))

    return lax.map(one, (q, k_new, v_new, page_table))


def project(h, w_in, b_conv):
    z = h @ w_in
    q, k, v, cg = jnp.split(z, [ATTN_WIDTH, 2 * ATTN_WIDTH, 3 * ATTN_WIDTH], axis=-1)
    a, g = jnp.split(cg + b_conv, 2, axis=-1)
    u = a * jax.nn.sigmoid(g)
    n, t = h.shape[0], h.shape[1]
    q = q.reshape(n, t, 2 * N_HEADS, QK_DIM)
    k = k.reshape(n, t, 2 * N_HEADS, QK_DIM)
    v = v.reshape(n, t, N_HEADS, HEAD_DIM)
    return q, k, v, u


def conv_branch(u_ext, dw_w, dw_b, ln_g, ln_b):
    y = lax.conv_general_dilated(
        u_ext, dw_w[:, None, :].astype(u_ext.dtype), window_strides=(1,), padding='VALID',
        dimension_numbers=('NWC', 'WIO', 'NWC'), feature_group_count=u_ext.shape[-1])
    y = y + dw_b
    return jax.nn.silu(layer_norm(y, ln_g, ln_b))


def merge(o_attn, sub_g, lam_init, y_conv, w_out):
    o = rms_norm(o_attn, sub_g) * (1.0 - lam_init)
    o = o.reshape(o.shape[0], o.shape[1], ATTN_WIDTH)
    return jnp.concatenate([o, y_conv], axis=-1) @ w_out


def moe(h, router_w, router_b, w_gu, b_gu, w_dn, b_dn):
    n_b, t, d = h.shape
    x = h.reshape(-1, d)
    n = x.shape[0]
    logits = (x @ router_w + router_b).astype(jnp.float32)
    top_v, top_i = lax.top_k(logits, TOP_K)
    gates = jax.nn.softmax(top_v, axis=-1)
    e_flat = top_i.reshape(-1)
    order = jnp.argsort(e_flat)
    e_s = e_flat[order]
    tok_s = order // TOP_K
    g_s = gates.reshape(-1)[order]
    counts = jnp.bincount(e_flat, length=N_EXPERTS)
    start = jnp.cumsum(counts) - counts
    padded = (counts + MOE_BLOCK - 1) // MOE_BLOCK * MOE_BLOCK
    pend = jnp.cumsum(padded)
    pstart = pend - padded
    r = n * TOP_K
    dest = pstart[e_s] + jnp.arange(r) - start[e_s]
    n_blk = -(-(r + N_EXPERTS * (MOE_BLOCK - 1)) // MOE_BLOCK)
    xp = jnp.zeros((n_blk * MOE_BLOCK, d), x.dtype).at[dest].set(x[tok_s])
    blk_e = jnp.minimum(jnp.searchsorted(pend, jnp.arange(n_blk) * MOE_BLOCK, side='right'),
                        N_EXPERTS - 1)

    def expert_block(args):
        xb, e = args
        gu = xb @ w_gu[e] + b_gu[e]
        gate, up = gu[:, ::2], gu[:, 1::2]
        gate = jnp.minimum(gate, SWIGLU_LIMIT)
        up = jnp.clip(up, -SWIGLU_LIMIT, SWIGLU_LIMIT)
        act = (up + 1.0) * (gate * jax.nn.sigmoid(SWIGLU_ALPHA * gate))
        return act @ w_dn[e] + b_dn[e]

    yp = lax.map(expert_block, (xp.reshape(n_blk, MOE_BLOCK, d), blk_e)).reshape(-1, d)
    y = jnp.zeros_like(x).at[tok_s].add(yp[dest] * g_s[:, None].astype(x.dtype))
    return y.reshape(n_b, t, d)


def setup_inputs(seed: int = 0) -> dict:
    key = jax.random.key(seed)
    ks = jax.random.split(key, 32)
    nrm = jax.random.normal
    f32 = jnp.float32
    n_pages = PAST_LEN // PAGE_SIZE
    n_used = DEC_BATCH * n_pages
    n_phys = n_used + max(1, n_used // 4)
    page_table = jax.random.permutation(ks[5], n_phys)[:n_used].reshape(DEC_BATCH, n_pages).astype(jnp.int32)
    return {
        "x_prompt": nrm(ks[0], (BATCH, SEQ, D_MODEL), f32),
        "x_sample": nrm(ks[1], (DEC_BATCH, DEC_SEQ, D_MODEL), f32),
        "cache_k": nrm(ks[2], (DEPTH, n_phys, PAGE_SIZE, 2 * N_HEADS, QK_DIM), f32),
        "cache_v": nrm(ks[3], (DEPTH, n_phys, PAGE_SIZE, N_HEADS, HEAD_DIM), f32),
        "state_conv": 0.5 * nrm(ks[4], (DEPTH, DEC_BATCH, CONV_K - 1, CONV_WIDTH), f32),
        "page_table": page_table,
        "meta_tokens": nrm(ks[6], (N_META, D_MODEL), f32),
        "norm_mix_g": 1.0 + 0.1 * nrm(ks[7], (DEPTH, D_MODEL), f32),
        "w_in": nrm(ks[8], (DEPTH, D_MODEL, D_IN), f32) * D_MODEL ** -0.5,
        "b_conv_in": 0.02 * nrm(ks[9], (DEPTH, 2 * CONV_WIDTH), f32),
        "lambda_q1": 0.1 * nrm(ks[10], (DEPTH, QK_DIM), f32),
        "lambda_k1": 0.1 * nrm(ks[11], (DEPTH, QK_DIM), f32),
        "lambda_q2": 0.1 * nrm(ks[12], (DEPTH, QK_DIM), f32),
        "lambda_k2": 0.1 * nrm(ks[13], (DEPTH, QK_DIM), f32),
        "subln_g": 1.0 + 0.1 * nrm(ks[14], (DEPTH, HEAD_DIM), f32),
        "dw_w": nrm(ks[15], (DEPTH, CONV_K, CONV_WIDTH), f32) * CONV_K ** -0.5,
        "dw_b": 0.02 * nrm(ks[16], (DEPTH, CONV_WIDTH), f32),
        "conv_ln_g": 1.0 + 0.1 * nrm(ks[17], (DEPTH, CONV_WIDTH), f32),
        "conv_ln_b": 0.02 * nrm(ks[18], (DEPTH, CONV_WIDTH), f32),
        "w_out": nrm(ks[19], (DEPTH, ATTN_WIDTH + CONV_WIDTH, D_MODEL), f32) * (ATTN_WIDTH + CONV_WIDTH) ** -0.5,
        "norm_ffn_g": 1.0 + 0.1 * nrm(ks[20], (DEPTH, D_MODEL), f32),
        "router_w": nrm(ks[21], (DEPTH, D_MODEL, N_EXPERTS), f32) * D_MODEL ** -0.5,
        "router_b": 0.01 * nrm(ks[22], (DEPTH, N_EXPERTS), f32),
        "w_gate_up": nrm(ks[23], (DEPTH, N_EXPERTS, D_MODEL, 2 * D_FF), f32) * D_MODEL ** -0.5,
        "b_gate_up": 0.02 * nrm(ks[24], (DEPTH, N_EXPERTS, 2 * D_FF), f32),
        "w_down": nrm(ks[25], (DEPTH, N_EXPERTS, D_FF, D_MODEL), f32) * D_FF ** -0.5,
        "b_down": 0.02 * nrm(ks[26], (DEPTH, N_EXPERTS, D_MODEL), f32),
        "final_norm_g": 1.0 + 0.1 * nrm(ks[27], (D_MODEL,), f32),
    }


def reference(x_prompt, x_sample, cache_k, cache_v, state_conv, page_table, meta_tokens,
              norm_mix_g, w_in, b_conv_in, lambda_q1, lambda_k1, lambda_q2, lambda_k2, subln_g,
              dw_w, dw_b, conv_ln_g, conv_ln_b, w_out, norm_ffn_g, router_w, router_b,
              w_gate_up, b_gate_up, w_down, b_down, final_norm_g):
    b, _, d = x_prompt.shape
    meta = jnp.broadcast_to(meta_tokens[None].astype(x_prompt.dtype), (b, N_META, d))
    xp = jnp.concatenate([meta, x_prompt], axis=1)
    xs = x_sample
    n_len = xp.shape[1]
    s_len = xs.shape[1]
    past_len = page_table.shape[1] * cache_k.shape[2]
    pos_p = jnp.arange(n_len)
    pos_s = past_len + jnp.arange(s_len)

    kp_rows, vp_rows, cp_rows, ks_rows, vs_rows, cs_rows = [], [], [], [], [], []
    for l in range(DEPTH):
        lam_init = 0.8 - 0.6 * math.exp(-0.3 * l)
        lam = diff_lambda(lambda_q1[l], lambda_k1[l], lambda_q2[l], lambda_k2[l], lam_init)

        qp_, kp_, vp_, up_ = project(rms_norm(xp, norm_mix_g[l]), w_in[l], b_conv_in[l])
        qs_, ks_, vs_, us_ = project(rms_norm(xs, norm_mix_g[l]), w_in[l], b_conv_in[l])
        qp_, kp_ = rope(qp_, pos_p), rope(kp_, pos_p)
        qs_, ks_ = rope(qs_, pos_s), rope(ks_, pos_s)

        o_p = diff_attn_prompt(qp_, kp_, vp_, lam)
        o_s = diff_attn_sample(qs_, ks_, vs_, cache_k, cache_v, l, page_table, lam)

        up_ext = jnp.pad(up_, ((0, 0), (CONV_K - 1, 0), (0, 0)))
        us_ext = jnp.concatenate([state_conv[l].astype(us_.dtype), us_], axis=1)
        yc_p = conv_branch(up_ext, dw_w[l], dw_b[l], conv_ln_g[l], conv_ln_b[l])
        yc_s = conv_branch(us_ext, dw_w[l], dw_b[l], conv_ln_g[l], conv_ln_b[l])

        xp = xp + merge(o_p, subln_g[l], lam_init, yc_p, w_out[l])
        xs = xs + merge(o_s, subln_g[l], lam_init, yc_s, w_out[l])

        xp = xp + moe(rms_norm(xp, norm_ffn_g[l]), router_w[l], router_b[l],
                      w_gate_up[l], b_gate_up[l], w_down[l], b_down[l])
        xs = xs + moe(rms_norm(xs, norm_ffn_g[l]), router_w[l], router_b[l],
                      w_gate_up[l], b_gate_up[l], w_down[l], b_down[l])

        kp_rows.append(kp_)
        vp_rows.append(vp_)
        cp_rows.append(up_ext[:, -(CONV_K - 1):])
        ks_rows.append(ks_)
        vs_rows.append(vs_)
        cs_rows.append(us_ext[:, -(CONV_K - 1):])

    y_prompt = rms_norm(xp[:, N_META:], final_norm_g)
    y_sample = rms_norm(xs, final_norm_g)
    new_k_prompt = jnp.stack(kp_rows)
    new_v_prompt = jnp.stack(vp_rows)
    new_conv_prompt = jnp.stack(cp_rows)
    new_k_sample = jnp.stack(ks_rows)
    new_v_sample = jnp.stack(vs_rows)
    new_conv_sample = jnp.stack(cs_rows)
    return (y_prompt, y_sample, new_k_prompt, new_v_prompt, new_conv_prompt,
            new_k_sample, new_v_sample, new_conv_sample)
```

```python
import functools
import math

import numpy as np
import jax
import jax.numpy as jnp
from jax import lax
from jax.experimental import pallas as pl
from jax.experimental.pallas import tpu as pltpu

F32 = jnp.float32
BF16 = jnp.bfloat16

N_META = 16
N_HEADS = 16
CONV_K = 31
TOP_K = 4
SWIGLU_LIMIT = 7.0
SWIGLU_ALPHA = 1.702
ROPE_THETA = 10000.0
NORM_EPS = 1e-6
LN_EPS = 1e-5
LANES = 128
VMEM_LIMIT = 56 * 1024 * 1024


def _cparams(*sem):
    return pltpu.CompilerParams(dimension_semantics=sem, vmem_limit_bytes=VMEM_LIMIT)


def _dot(a, b):
    return jnp.dot(a, b, preferred_element_type=F32)


def _dot_nt(a, b):
    return lax.dot_general(a, b, (((1,), (1,)), ((), ())), preferred_element_type=F32)


def _split(x):
    hi = x.astype(BF16)
    lo = (x - hi.astype(F32)).astype(BF16)
    return hi, lo


def _rms(x, g, eps):
    return x * lax.rsqrt(jnp.mean(x * x, axis=-1, keepdims=True) + eps) * g


def _rmsnorm_body(x_ref, g_ref, o_ref, *, eps):
    o_ref[...] = _rms(x_ref[...], g_ref[...], eps).astype(o_ref.dtype)


def _rmsnorm(x, g, out_dtype, tm):
    m, d = x.shape
    return pl.pallas_call(
        functools.partial(_rmsnorm_body, eps=NORM_EPS),
        grid=(pl.cdiv(m, tm),),
        in_specs=[pl.BlockSpec((tm, d), lambda i: (i, 0)), pl.BlockSpec((1, d), lambda i: (0, 0))],
        out_specs=pl.BlockSpec((tm, d), lambda i: (i, 0)),
        out_shape=jax.ShapeDtypeStruct((m, d), out_dtype),
        compiler_params=_cparams("parallel"),
        name="rmsnorm",
    )(x, g.reshape(1, d))


def _mm(x_refs, w_ref):
    z = None
    r0 = 0
    for xr in x_refs:
        kk = xr.shape[1]
        zp = _dot(xr[...], w_ref[r0:r0 + kk, :])
        z = zp if z is None else z + zp
        r0 += kk
    return z


def _rope_body(x_ref, w_ref, c_ref, s_ref, *o_refs, scale, tn):
    z = _mm([x_ref], w_ref)
    c = c_ref[...]
    s = s_ref[...]
    lane = lax.broadcasted_iota(jnp.int32, (1, LANES), 1)
    first = (lane % 64) < 32
    for j in range(tn // LANES):
        zj = z[:, LANES * j:LANES * (j + 1)]
        rot = jnp.where(first, pltpu.roll(zj, LANES - 32, 1), pltpu.roll(zj, 32, 1))
        r = zj * c + rot * s
        if scale != 1.0:
            r = r * scale
        for o in o_refs:
            o[:, LANES * j:LANES * (j + 1)] = r.astype(o.dtype)


def _plain_body(x_ref, w_ref, *o_refs):
    z = _mm([x_ref], w_ref)
    for o in o_refs:
        o[...] = z.astype(o.dtype)


def _glu_body(x_ref, wa_ref, wg_ref, ba_ref, bg_ref, o_ref):
    a = _mm([x_ref], wa_ref) + ba_ref[...]
    g = _mm([x_ref], wg_ref) + bg_ref[...]
    o_ref[...] = a * jax.nn.sigmoid(g)


def _res_body(*refs, nparts):
    x_refs = refs[:nparts]
    w_ref, r_ref, o_ref = refs[nparts:]
    o_ref[...] = _mm(x_refs, w_ref) + r_ref[...]


def _proj_rope(x, w, col0, ncols, cs, sn, *, scale, out_dtypes, tm, tn, tab_blocks):
    m, k = x.shape
    cb = col0 // tn
    return pl.pallas_call(
        functools.partial(_rope_body, scale=scale, tn=tn),
        grid=(m // tm, ncols // tn),
        in_specs=[pl.BlockSpec((tm, k), lambda i, j: (i, 0)),
                  pl.BlockSpec((k, tn), lambda i, j: (0, cb + j)),
                  pl.BlockSpec((tm, LANES), lambda i, j: (i % tab_blocks, 0)),
                  pl.BlockSpec((tm, LANES), lambda i, j: (i % tab_blocks, 0))],
        out_specs=[pl.BlockSpec((tm, tn), lambda i, j: (i, j)) for _ in out_dtypes],
        out_shape=[jax.ShapeDtypeStruct((m, ncols), dt) for dt in out_dtypes],
        compiler_params=_cparams("parallel", "arbitrary"),
        name="proj_rope",
    )(x, w, cs, sn)


def _proj_plain(x, w, col0, ncols, *, out_dtypes, tm, tn):
    m, k = x.shape
    cb = col0 // tn
    return pl.pallas_call(
        _plain_body,
        grid=(m // tm, ncols // tn),
        in_specs=[pl.BlockSpec((tm, k), lambda i, j: (i, 0)),
                  pl.BlockSpec((k, tn), lambda i, j: (0, cb + j))],
        out_specs=[pl.BlockSpec((tm, tn), lambda i, j: (i, j)) for _ in out_dtypes],
        out_shape=[jax.ShapeDtypeStruct((m, ncols), dt) for dt in out_dtypes],
        compiler_params=_cparams("parallel", "arbitrary"),
        name="proj_plain",
    )(x, w)


def _proj_glu(x, w, col_a, col_g, ncols, bias, *, tm, tn):
    m, k = x.shape
    ca, cg = col_a // tn, col_g // tn
    nb = ncols // tn
    b2 = bias.reshape(1, 2 * ncols)
    return pl.pallas_call(
        _glu_body,
        grid=(m // tm, nb),
        in_specs=[pl.BlockSpec((tm, k), lambda i, j: (i, 0)),
                  pl.BlockSpec((k, tn), lambda i, j: (0, ca + j)),
                  pl.BlockSpec((k, tn), lambda i, j: (0, cg + j)),
                  pl.BlockSpec((1, tn), lambda i, j: (0, j)),
                  pl.BlockSpec((1, tn), lambda i, j: (0, nb + j))],
        out_specs=pl.BlockSpec((tm, tn), lambda i, j: (i, j)),
        out_shape=jax.ShapeDtypeStruct((m, ncols), F32),
        compiler_params=_cparams("parallel", "arbitrary"),
        name="proj_glu",
    )(x, w, w, b2, b2)


def _proj_res(xs, w, res, *, tm, tn):
    m = xs[0].shape[0]
    k, n = w.shape
    return pl.pallas_call(
        functools.partial(_res_body, nparts=len(xs)),
        grid=(m // tm, n // tn),
        in_specs=[pl.BlockSpec((tm, x.shape[1]), lambda i, j: (i, 0)) for x in xs]
                 + [pl.BlockSpec((k, tn), lambda i, j: (0, j)),
                    pl.BlockSpec((tm, tn), lambda i, j: (i, j))],
        out_specs=pl.BlockSpec((tm, tn), lambda i, j: (i, j)),
        out_shape=jax.ShapeDtypeStruct((m, n), F32),
        compiler_params=_cparams("parallel", "arbitrary"),
        name="proj_res",
    )(*xs, w, res)


def _rope_tables(pos, d):
    inv = ROPE_THETA ** (-jnp.arange(0, d, 2, dtype=F32) / d)
    ang = pos.astype(F32)[:, None] * inv[None, :]
    cos, sin = jnp.cos(ang), jnp.sin(ang)
    reps = LANES // d
    return (jnp.concatenate([cos, cos] * reps, axis=-1),
            jnp.concatenate([-sin, sin] * reps, axis=-1))


def _diff_lambda(lq1, lk1, lq2, lk2, lam_init):
    return (jnp.exp(jnp.sum(lq1[...] * lk1[...], keepdims=True))
            - jnp.exp(jnp.sum(lq2[...] * lk2[...], keepdims=True)) + lam_init)


def _attn_body(lq1, lk1, lq2, lk2, g_ref, q_ref, k_ref, v_ref, km_ref, vm_ref, o_ref,
               m_ref, l_ref, acc_ref, *, tq, n_meta, lam_init):
    qi = pl.program_id(2)
    q = q_ref[0]
    lane = lax.broadcasted_iota(jnp.int32, (1, LANES), 1)
    zq = jnp.zeros_like(q)
    qq = jnp.concatenate([jnp.where(lane < 64, q, zq), jnp.where(lane >= 64, q, zq)], axis=0)

    s = _dot_nt(qq, km_ref[...])
    s = jnp.where(lane < n_meta, s, -jnp.inf)
    m0 = jnp.max(s, axis=1, keepdims=True)
    p0 = jnp.exp(s - m0)
    m_ref[...] = m0
    l_ref[...] = jnp.sum(p0, axis=1, keepdims=True)
    acc_ref[...] = _dot(p0.astype(BF16), vm_ref[...])

    def update(s, v):
        m_prev = m_ref[...]
        m_new = jnp.maximum(m_prev, jnp.max(s, axis=1, keepdims=True))
        alpha = jnp.exp(m_prev - m_new)
        p = jnp.exp(s - m_new)
        l_ref[...] = alpha * l_ref[...] + jnp.sum(p, axis=1, keepdims=True)
        acc_ref[...] = alpha * acc_ref[...] + _dot(p.astype(BF16), v)
        m_ref[...] = m_new

    def body(kb, c):
        off = pl.multiple_of(kb * tq, tq)
        update(_dot_nt(qq, k_ref[0, pl.ds(off, tq), :]), v_ref[0, pl.ds(off, tq), :])
        return c

    lax.fori_loop(0, qi, body, 0)

    off = pl.multiple_of(qi * tq, tq)
    s = _dot_nt(qq, k_ref[0, pl.ds(off, tq), :])
    row = lax.broadcasted_iota(jnp.int32, (2 * tq, tq), 0)
    col = lax.broadcasted_iota(jnp.int32, (2 * tq, tq), 1)
    row = jnp.where(row >= tq, row - tq, row)
    update(jnp.where(col <= row, s, -jnp.inf), v_ref[0, pl.ds(off, tq), :])

    lam = _diff_lambda(lq1, lk1, lq2, lk2, lam_init)
    o = acc_ref[...] / l_ref[...]
    o = o[:tq] - lam * o[tq:]
    o = _rms(o, g_ref[...], NORM_EPS) * (1.0 - lam_init)
    o_ref[0] = o.astype(o_ref.dtype)


def _attn_prompt(q, k, v, km, vm, lams, sub_g, lam_init, tq):
    b, s, w = q.shape
    nh = w // LANES
    vec = lambda a: a.reshape(1, -1)
    small = pl.BlockSpec((1, lams[0].shape[-1]), lambda bi, h, i: (0, 0))
    return pl.pallas_call(
        functools.partial(_attn_body, tq=tq, n_meta=N_META, lam_init=lam_init),
        grid=(b, nh, s // tq),
        in_specs=[small, small, small, small,
                  pl.BlockSpec((1, LANES), lambda bi, h, i: (0, 0)),
                  pl.BlockSpec((1, tq, LANES), lambda bi, h, i: (bi, i, h)),
                  pl.BlockSpec((1, s, LANES), lambda bi, h, i: (bi, 0, h)),
                  pl.BlockSpec((1, s, LANES), lambda bi, h, i: (bi, 0, h)),
                  pl.BlockSpec((LANES, LANES), lambda bi, h, i: (0, h)),
                  pl.BlockSpec((LANES, LANES), lambda bi, h, i: (0, h))],
        out_specs=pl.BlockSpec((1, tq, LANES), lambda bi, h, i: (bi, i, h)),
        out_shape=jax.ShapeDtypeStruct((b, s, w), BF16),
        scratch_shapes=[pltpu.VMEM((2 * tq, 1), F32), pltpu.VMEM((2 * tq, 1), F32),
                        pltpu.VMEM((2 * tq, LANES), F32)],
        compiler_params=_cparams("parallel", "parallel", "arbitrary"),
        name="attn_prompt",
    )(*[vec(a) for a in lams], vec(sub_g), q, k, v, km, vm)


def _ln_silu(y, g, b):
    mu = jnp.mean(y, axis=-1, keepdims=True)
    d = y - mu
    var = jnp.mean(d * d, axis=-1, keepdims=True)
    z = d * lax.rsqrt(var + LN_EPS) * g + b
    return z * jax.nn.sigmoid(z)


def _conv_body(head_ref, prev_ref, cur_ref, w_ref, b_ref, g_ref, bb_ref, o_ref, ext_ref, y_ref, *, tr, kc, halo):
    i = pl.program_id(1)

    @pl.when(i == 0)
    def _():
        ext_ref[0:halo, :] = head_ref[...]

    @pl.when(i > 0)
    def _():
        ext_ref[0:halo, :] = prev_ref[...]

    ext_ref[halo:halo + tr, :] = cur_ref[...]
    c = cur_ref.shape[1]
    rc, cc = 64, 256
    off0 = halo - (kc - 1)
    for r0 in range(0, tr, rc):
        for c0 in range(0, c, cc):
            acc = jnp.zeros((rc, cc), F32)
            for j in range(kc):
                acc = acc + ext_ref[r0 + off0 + j:r0 + off0 + j + rc, c0:c0 + cc] * w_ref[j:j + 1, c0:c0 + cc]
            y_ref[r0:r0 + rc, c0:c0 + cc] = acc + b_ref[:, c0:c0 + cc]
    o_ref[...] = _ln_silu(y_ref[...], g_ref[...], bb_ref[...]).astype(o_ref.dtype)


def _conv_prompt(u, head, dw_w, dw_b, ln_g, ln_b, nb, tr):
    rows, c = u.shape
    s = rows // nb
    halo = head.shape[0]
    kc = dw_w.shape[0]
    vec = lambda a: a.reshape(1, c)
    cst = lambda bi, i: (0, 0)
    return pl.pallas_call(
        functools.partial(_conv_body, tr=tr, kc=kc, halo=halo),
        grid=(nb, s // tr),
        in_specs=[pl.BlockSpec((halo, c), cst),
                  pl.BlockSpec((halo, c), lambda bi, i: (jnp.maximum((bi * s + i * tr) // halo - 1, 0), 0)),
                  pl.BlockSpec((tr, c), lambda bi, i: (bi * (s // tr) + i, 0)),
                  pl.BlockSpec((kc, c), cst), pl.BlockSpec((1, c), cst),
                  pl.BlockSpec((1, c), cst), pl.BlockSpec((1, c), cst)],
        out_specs=pl.BlockSpec((tr, c), lambda bi, i: (bi * (s // tr) + i, 0)),
        out_shape=jax.ShapeDtypeStruct((rows, c), BF16),
        scratch_shapes=[pltpu.VMEM((halo + tr, c), F32), pltpu.VMEM((tr, c), F32)],
        compiler_params=_cparams("parallel", "arbitrary"),
        name="conv_prompt",
    )(head, u, u, dw_w, vec(dw_b), vec(ln_g), vec(ln_b))


def _sconv_body(st_ref, u_ref, w_ref, b_ref, g_ref, bb_ref, o_ref, *, kc):
    w = w_ref[...]
    y = jnp.sum(st_ref[...] * w[None, :kc - 1, :], axis=1) + u_ref[...] * w[kc - 1:kc, :] + b_ref[...]
    o_ref[...] = _ln_silu(y, g_ref[...], bb_ref[...])


def _conv_sample(state, u, dw_w, dw_b, ln_g, ln_b):
    nb, _, c = state.shape
    vec = lambda a: a.reshape(1, c)
    return pl.pallas_call(
        functools.partial(_sconv_body, kc=dw_w.shape[0]),
        out_shape=jax.ShapeDtypeStruct((nb, c), F32),
        compiler_params=pltpu.CompilerParams(vmem_limit_bytes=VMEM_LIMIT),
        name="conv_sample",
    )(state, u, dw_w, vec(dw_b), vec(ln_g), vec(ln_b))


def _bf(x):
    return x.astype(BF16).astype(F32)


def _sattn_body(pt_ref, lq1, lk1, lq2, lk2, g_ref, q_ref, kn_ref, vn_ref, kc_ref, vc_ref, o_ref,
                m_ref, l_ref, acc_ref, *, nh, npg, lam_init):
    p = pl.program_id(1)
    sets = (pl.ds(0, nh, stride=2), pl.ds(1, nh, stride=2))

    def scores(i):
        return jnp.sum(_bf(kc_ref[0, :, sets[i], :]) * _bf(q_ref[0, sets[i], :])[None],
                       axis=-1, keepdims=True)

    def score_new(i):
        return jnp.sum(kn_ref[0, sets[i], :] * q_ref[0, sets[i], :], axis=-1, keepdims=True)

    @pl.when(p == 0)
    def _():
        m_ref[...] = jnp.full(m_ref.shape, -jnp.inf, F32)
        l_ref[...] = jnp.zeros(l_ref.shape, F32)
        acc_ref[...] = jnp.zeros(acc_ref.shape, F32)

    def stats(i, s, smax):
        m_prev = m_ref[i]
        m_new = jnp.maximum(m_prev, smax)
        l_ref[i] = jnp.exp(m_prev - m_new) * l_ref[i] + jnp.sum(jnp.exp(s - m_new), axis=0)
        m_ref[i] = m_new

    @pl.when(p < npg)
    def _():
        for i in range(2):
            s = scores(i)
            stats(i, s, jnp.max(s, axis=0))

    @pl.when(p == npg - 1)
    def _():
        for i in range(2):
            s = score_new(i)
            stats(i, s[None], s)

    lam = _diff_lambda(lq1, lk1, lq2, lk2, lam_init)

    @pl.when(p >= npg)
    def _():
        a = (jnp.exp(scores(0) - m_ref[0]) / l_ref[0]
             - lam * (jnp.exp(scores(1) - m_ref[1]) / l_ref[1]))
        acc_ref[...] += jnp.sum(_bf(a) * _bf(vc_ref[0]), axis=0)

    @pl.when(p == 2 * npg - 1)
    def _():
        a = (jnp.exp(score_new(0) - m_ref[0]) / l_ref[0]
             - lam * (jnp.exp(score_new(1) - m_ref[1]) / l_ref[1]))
        o = acc_ref[...] + a * vn_ref[0]
        o_ref[0] = _rms(o, g_ref[...], NORM_EPS) * (1.0 - lam_init)


def _attn_sample(q, kn, vn, cache_k, cache_v, page_table, lams, sub_g, lam_init):
    nb, nsub, dq = q.shape
    nh = nsub // 2
    dv = vn.shape[-1]
    npg = page_table.shape[1]
    page = cache_k.shape[1]
    vec = lambda a: a.reshape(1, -1)
    small = pl.BlockSpec((1, lams[0].shape[-1]), lambda b, p, pt: (0, 0))
    grid_spec = pltpu.PrefetchScalarGridSpec(
        num_scalar_prefetch=1,
        grid=(nb, 2 * npg),
        in_specs=[small, small, small, small,
                  pl.BlockSpec((1, dv), lambda b, p, pt: (0, 0)),
                  pl.BlockSpec((1, nsub, dq), lambda b, p, pt: (b, 0, 0)),
                  pl.BlockSpec((1, nsub, dq), lambda b, p, pt: (b, 0, 0)),
                  pl.BlockSpec((1, nh, dv), lambda b, p, pt: (b, 0, 0)),
                  pl.BlockSpec((1, page, nsub, dq), lambda b, p, pt: (pt[b * npg + p % npg], 0, 0, 0)),
                  pl.BlockSpec((1, page, nh, dv),
                               lambda b, p, pt: (pt[b * npg + jnp.maximum(p - npg, 0)], 0, 0, 0))],
        out_specs=pl.BlockSpec((1, nh, dv), lambda b, p, pt: (b, 0, 0)),
        scratch_shapes=[pltpu.VMEM((2, nh, 1), F32), pltpu.VMEM((2, nh, 1), F32),
                        pltpu.VMEM((nh, dv), F32)],
    )
    return pl.pallas_call(
        functools.partial(_sattn_body, nh=nh, npg=npg, lam_init=lam_init),
        grid_spec=grid_spec,
        out_shape=jax.ShapeDtypeStruct((nb, nh, dv), F32),
        compiler_params=_cparams("parallel", "arbitrary"),
        name="attn_sample",
    )(page_table.reshape(-1), *[vec(a) for a in lams], vec(sub_g), q, kn, vn, cache_k, cache_v)


def _router_body(x_ref, g_ref, w_ref, b_ref, gate_ref, idx_ref, *, topk):
    tm = x_ref.shape[0]
    h = _rms(x_ref[...], g_ref[...], NORM_EPS)
    logits = _dot(h.astype(BF16), w_ref[...].astype(BF16)) + b_ref[...]
    ne = logits.shape[1]
    eidx = lax.broadcasted_iota(jnp.int32, logits.shape, 1)
    vals, idxs = [], []
    cur = logits
    for _ in range(topk):
        mx = jnp.max(cur, axis=-1, keepdims=True)
        ix = jnp.min(jnp.where(cur == mx, eidx, ne), axis=-1, keepdims=True)
        vals.append(mx)
        idxs.append(ix)
        cur = jnp.where(eidx == ix, -jnp.inf, cur)
    ex = [jnp.exp(v - vals[0]) for v in vals]
    tot = ex[0]
    for e in ex[1:]:
        tot = tot + e
    lane = lax.broadcasted_iota(jnp.int32, (tm, LANES), 1)
    gates = jnp.zeros((tm, LANES), F32)
    ids = jnp.zeros((tm, LANES), jnp.int32)
    for kk in range(topk):
        gates = jnp.where(lane == kk, ex[kk] / tot, gates)
        ids = jnp.where(lane == kk, idxs[kk], ids)
    gate_ref[...] = gates
    idx_ref[...] = ids


def _router(x, g, w, b, tm):
    t, d = x.shape
    ne = w.shape[1]
    return pl.pallas_call(
        functools.partial(_router_body, topk=TOP_K),
        grid=(pl.cdiv(t, tm),),
        in_specs=[pl.BlockSpec((tm, d), lambda i: (i, 0)), pl.BlockSpec((1, d), lambda i: (0, 0)),
                  pl.BlockSpec((d, ne), lambda i: (0, 0)), pl.BlockSpec((1, ne), lambda i: (0, 0))],
        out_specs=[pl.BlockSpec((tm, LANES), lambda i: (i, 0)), pl.BlockSpec((tm, LANES), lambda i: (i, 0))],
        out_shape=[jax.ShapeDtypeStruct((t, LANES), F32), jax.ShapeDtypeStruct((t, LANES), jnp.int32)],
        compiler_params=_cparams("parallel"),
        name="router",
    )(x, g.reshape(1, d), w, b.reshape(1, ne))


def _gather_body(src_ref, nact_ref, x_hbm, g_ref, o_ref, buf, sem, *, tm):
    i = pl.program_id(0)

    @pl.when(i < nact_ref[0])
    def _():
        def issue(r, c):
            tok = src_ref[i * tm + r]
            pltpu.make_async_copy(x_hbm.at[pl.ds(tok, 1)], buf.at[pl.ds(r, 1)], sem).start()
            return c

        lax.fori_loop(0, tm, issue, 0)
        pltpu.make_async_copy(x_hbm.at[pl.ds(0, tm)], buf, sem).wait()
        o_ref[...] = _rms(buf[...], g_ref[...], NORM_EPS).astype(o_ref.dtype)

    @pl.when(i >= nact_ref[0])
    def _():
        o_ref[...] = jnp.zeros(o_ref.shape, o_ref.dtype)


def _moe_gather(x, g, src_tok, n_tiles, ntile_max, tm):
    d = x.shape[1]
    grid_spec = pltpu.PrefetchScalarGridSpec(
        num_scalar_prefetch=2,
        grid=(ntile_max,),
        in_specs=[pl.BlockSpec(memory_space=pl.ANY), pl.BlockSpec((1, d), lambda i, s, n: (0, 0))],
        out_specs=pl.BlockSpec((tm, d), lambda i, s, n: (i, 0)),
        scratch_shapes=[pltpu.VMEM((tm, d), F32), pltpu.SemaphoreType.DMA(())],
    )
    return pl.pallas_call(
        functools.partial(_gather_body, tm=tm),
        grid_spec=grid_spec,
        out_shape=jax.ShapeDtypeStruct((ntile_max * tm, d), BF16),
        compiler_params=_cparams("arbitrary"),
        name="moe_gather",
    )(src_tok, n_tiles.reshape(1), x, g.reshape(1, d))


def _deinterleave_matrix(n):
    p = np.zeros((n, n), np.float32)
    half = n // 2
    p[2 * np.arange(half), np.arange(half)] = 1.0
    p[2 * np.arange(half) + 1, half + np.arange(half)] = 1.0
    return jnp.asarray(p, BF16)


def _moe_mm_body(e_ref, c_ref, t_ref, f_ref, nact_ref, x_ref, w_ref, b_ref, *rest, swiglu, pw):
    if swiglu:
        p_ref, o_ref, wbf = rest
    else:
        o_ref, wbf = rest
    i = pl.program_id(0)

    @pl.when(i < nact_ref[0])
    def _():
        @pl.when(f_ref[i] == 1)
        def _():
            wbf[...] = w_ref[0].astype(BF16)

        z = _dot(x_ref[...], wbf[...]) + b_ref[0]
        if not swiglu:
            o_ref[...] = z
            return
        half = pw // 2
        for j in range(z.shape[1] // pw):
            hi, lo = _split(z[:, pw * j:pw * (j + 1)])
            d = _dot(hi, p_ref[...]) + _dot(lo, p_ref[...])
            gate = jnp.minimum(d[:, :half], SWIGLU_LIMIT)
            up = jnp.clip(d[:, half:], -SWIGLU_LIMIT, SWIGLU_LIMIT)
            act = (up + 1.0) * (gate * jax.nn.sigmoid(SWIGLU_ALPHA * gate))
            o_ref[:, half * j:half * (j + 1)] = act.astype(o_ref.dtype)

    @pl.when(i >= nact_ref[0])
    def _():
        o_ref[...] = jnp.zeros(o_ref.shape, o_ref.dtype)


def _moe_mm(x, w, b, tabs, *, swiglu, tm, tn):
    e_tab, c_tab, t_tab, f_tab, nact = tabs
    ne, k, n = w.shape
    nsteps = e_tab.shape[0]
    rows = x.shape[0]
    n_out = n // 2 if swiglu else n
    tn_out = tn // 2 if swiglu else tn
    pw = 2 * LANES
    in_specs = [pl.BlockSpec((tm, k), lambda i, e, c, t, f, a: (t[i], 0)),
                pl.BlockSpec((1, k, tn), lambda i, e, c, t, f, a: (e[i], 0, c[i])),
                pl.BlockSpec((1, 1, tn), lambda i, e, c, t, f, a: (e[i], 0, c[i]))]
    args = [x, w, b.reshape(ne, 1, n)]
    if swiglu:
        in_specs.append(pl.BlockSpec((pw, pw), lambda i, e, c, t, f, a: (0, 0)))
        args.append(_deinterleave_matrix(pw))
    grid_spec = pltpu.PrefetchScalarGridSpec(
        num_scalar_prefetch=5,
        grid=(nsteps,),
        in_specs=in_specs,
        out_specs=pl.BlockSpec((tm, tn_out), lambda i, e, c, t, f, a: (t[i], c[i])),
        scratch_shapes=[pltpu.VMEM((k, tn), BF16)],
    )
    return pl.pallas_call(
        functools.partial(_moe_mm_body, swiglu=swiglu, pw=pw),
        grid_spec=grid_spec,
        out_shape=jax.ShapeDtypeStruct((rows, n_out), BF16 if swiglu else F32),
        compiler_params=_cparams("arbitrary"),
        name="moe_gate_up" if swiglu else "moe_down",
    )(e_tab, c_tab, t_tab, f_tab, nact, *args)


def _combine_body(dest_ref, x_ref, gate_ref, ys_hbm, fg_ref, o_ref, buf, sem, *, tt, topk):
    i = pl.program_id(0)

    def issue(r, c):
        for kk in range(topk):
            d = dest_ref[(i * tt + r) * topk + kk]
            pltpu.make_async_copy(ys_hbm.at[pl.ds(d, 1)], buf.at[kk, pl.ds(r, 1)], sem).start()
        return c

    lax.fori_loop(0, tt, issue, 0)
    for kk in range(topk):
        pltpu.make_async_copy(ys_hbm.at[pl.ds(0, tt)], buf.at[kk], sem).wait()
    gates = gate_ref[...]
    y = x_ref[...]
    for kk in range(topk):
        y = y + gates[:, kk:kk + 1] * buf[kk]
    o_ref[...] = _rms(y, fg_ref[...], NORM_EPS)


def _moe_combine(x, gates, dest, ys, fg, tt):
    t, d = x.shape
    assert t % tt == 0
    grid_spec = pltpu.PrefetchScalarGridSpec(
        num_scalar_prefetch=1,
        grid=(t // tt,),
        in_specs=[pl.BlockSpec((tt, d), lambda i, s: (i, 0)),
                  pl.BlockSpec((tt, LANES), lambda i, s: (i, 0)),
                  pl.BlockSpec(memory_space=pl.ANY),
                  pl.BlockSpec((1, d), lambda i, s: (0, 0))],
        out_specs=pl.BlockSpec((tt, d), lambda i, s: (i, 0)),
        scratch_shapes=[pltpu.VMEM((TOP_K, tt, d), F32), pltpu.SemaphoreType.DMA(())],
    )
    return pl.pallas_call(
        functools.partial(_combine_body, tt=tt, topk=TOP_K),
        grid_spec=grid_spec,
        out_shape=jax.ShapeDtypeStruct((t, d), F32),
        compiler_params=_cparams("arbitrary"),
        name="moe_combine",
    )(dest, x, gates, ys, fg.reshape(1, d))


def _route_tables(top_i, ne, tm, ntile_max):
    t, topk = top_i.shape
    r = t * topk
    e_flat = top_i.reshape(-1)
    counts = jnp.zeros((ne,), jnp.int32).at[e_flat].add(1)
    ntile = (counts + tm - 1) // tm
    tend = jnp.cumsum(ntile)
    tstart = tend - ntile
    order = jnp.argsort(e_flat, stable=True).astype(jnp.int32)
    e_s = e_flat[order]
    start = jnp.cumsum(counts) - counts
    dest_sorted = tstart[e_s] * tm + jnp.arange(r, dtype=jnp.int32) - start[e_s]
    dest = jnp.zeros((r,), jnp.int32).at[order].set(dest_sorted)
    src_tok = jnp.zeros((ntile_max * tm,), jnp.int32).at[dest_sorted].set(order // topk)
    return dest, src_tok, ntile, tstart, tend[-1]


def _step_tables(ntile, tstart, nchunk, ntile_max):
    per = ntile * nchunk
    send = jnp.cumsum(per)
    sstart = send - per
    nact = send[-1]
    n_tiles = nact // nchunk
    i = jnp.arange(ntile_max * nchunk, dtype=jnp.int32)
    active = i < nact
    ic = jnp.minimum(i, nact - 1)
    e = jnp.searchsorted(send, ic, side="right").astype(jnp.int32)
    local = ic - sstart[e]
    nt = jnp.maximum(ntile[e], 1)
    tail = i - nact
    c = jnp.where(active, local // nt, tail % nchunk)
    tile = jnp.where(active, tstart[e] + local % nt, n_tiles + tail // nchunk)
    first = ((local % nt == 0) & active).astype(jnp.int32)
    return (e, c.astype(jnp.int32), tile.astype(jnp.int32), first, nact.reshape(1).astype(jnp.int32))


def kernel(x_prompt, x_sample, cache_k, cache_v, state_conv, page_table, meta_tokens, norm_mix_g, w_in, b_conv_in,
           lambda_q1, lambda_k1, lambda_q2, lambda_k2, subln_g, dw_w, dw_b, conv_ln_g, conv_ln_b, w_out, norm_ffn_g,
           router_w, router_b, w_gate_up, b_gate_up, w_down, b_down, final_norm_g):
    nb, seq, d = x_prompt.shape
    ns = x_sample.shape[0]
    depth = w_in.shape[0]
    assert depth == 1 and x_sample.shape[1] == 1
    aw = cache_v.shape[-1] * cache_v.shape[-2]
    cw = state_conv.shape[-1]
    nh = cache_v.shape[-2]
    dq = cache_k.shape[-1]
    ne = router_w.shape[-1]
    lam_init = 0.8 - 0.6 * math.exp(-0.3 * 0)
    scale = dq ** -0.5
    lams = (lambda_q1[0], lambda_k1[0], lambda_q2[0], lambda_k2[0])
    w_in0, w_out0 = w_in[0], w_out[0]
    rows = nb * seq
    nsm = N_META + ns

    tm = min(1024, seq)
    tn = 512
    tq = min(512, seq)

    w_in_b = w_in0.astype(BF16)
    xn = _rmsnorm(x_prompt.reshape(rows, d), norm_mix_g[0], BF16, 256)
    cos_p, sin_p = _rope_tables(N_META + jnp.arange(seq), dq)
    (q_b,) = _proj_rope(xn, w_in_b, 0, aw, cos_p, sin_p, scale=scale, out_dtypes=(BF16,),
                        tm=tm, tn=tn, tab_blocks=seq // tm)
    k_f, k_b = _proj_rope(xn, w_in_b, aw, aw, cos_p, sin_p, scale=1.0, out_dtypes=(F32, BF16),
                          tm=tm, tn=tn, tab_blocks=seq // tm)
    v_f, v_b = _proj_plain(xn, w_in_b, 2 * aw, aw, out_dtypes=(F32, BF16), tm=tm, tn=tn)
    u_p = _proj_glu(xn, w_in_b, 3 * aw, 3 * aw + cw, cw, b_conv_in[0], tm=tm, tn=tn)

    x_small = jnp.concatenate([meta_tokens, x_sample[:, 0, :]], axis=0)
    xn_s = _rmsnorm(x_small, norm_mix_g[0], BF16, nsm)
    pos_s = jnp.concatenate([jnp.arange(N_META), jnp.full((ns,), page_table.shape[1] * cache_k.shape[2])])
    cos_s, sin_s = _rope_tables(pos_s, dq)
    (q_s,) = _proj_rope(xn_s, w_in_b, 0, aw, cos_s, sin_s, scale=scale, out_dtypes=(F32,),
                        tm=nsm, tn=tn, tab_blocks=1)
    (k_s,) = _proj_rope(xn_s, w_in_b, aw, aw, cos_s, sin_s, scale=1.0, out_dtypes=(F32,),
                        tm=nsm, tn=tn, tab_blocks=1)
    (v_s,) = _proj_plain(xn_s, w_in_b, 2 * aw, aw, out_dtypes=(F32,), tm=nsm, tn=tn)
    u_s = _proj_glu(xn_s, w_in_b, 3 * aw, 3 * aw + cw, cw, b_conv_in[0], tm=nsm, tn=tn)
    k_meta, v_meta, u_meta = k_s[:N_META], v_s[:N_META], u_s[:N_META]
    q_smp, k_smp, v_smp, u_smp = q_s[N_META:], k_s[N_META:], v_s[N_META:], u_s[N_META:]

    pad = ((0, LANES - N_META), (0, 0))
    km = jnp.pad(k_meta, pad).astype(BF16)
    vm = jnp.pad(v_meta, pad).astype(BF16)
    o_p = _attn_prompt(q_b.reshape(nb, seq, aw), k_b.reshape(nb, seq, aw), v_b.reshape(nb, seq, aw),
                       km, vm, lams, subln_g[0], lam_init, tq)

    halo = 32
    head = jnp.concatenate([jnp.zeros((halo - N_META, cw), F32), u_meta], axis=0)
    yc_p = _conv_prompt(u_p, head, dw_w[0], dw_b[0], conv_ln_g[0], conv_ln_b[0], nb, 128)

    w_out_b = w_out0.astype(BF16)
    x1_p = _proj_res([o_p.reshape(rows, aw), yc_p], w_out_b, x_prompt.reshape(rows, d), tm=tm, tn=tn)

    o_s = _attn_sample(q_smp.reshape(ns, 2 * nh, dq), k_smp.reshape(ns, 2 * nh, dq), v_smp.reshape(ns, nh, aw // nh),
                       cache_k[0], cache_v[0], page_table, lams, subln_g[0], lam_init)
    yc_s = _conv_sample(state_conv[0], u_smp, dw_w[0], dw_b[0], conv_ln_g[0], conv_ln_b[0])
    x1_s = _proj_res([jnp.concatenate([o_s.reshape(ns, aw), yc_s], axis=1).astype(BF16)], w_out_b,
                     x_sample[:, 0, :], tm=ns, tn=tn)

    x1 = jnp.concatenate([x1_p, x1_s], axis=0)
    t_all = rows + ns
    gates, top_i = _router(x1, norm_ffn_g[0], router_w[0], router_b[0], 256)
    tmm = 256
    ntile_max = -(-(t_all * TOP_K + ne * (tmm - 1)) // tmm)
    dest, src_tok, ntile, tstart, n_tiles = _route_tables(top_i[:, :TOP_K], ne, tmm, ntile_max)
    xs = _moe_gather(x1, norm_ffn_g[0], src_tok, n_tiles, ntile_max, tmm)
    tn_gu = 512
    tn_dn = 512
    tabs_gu = _step_tables(ntile, tstart, w_gate_up.shape[-1] // tn_gu, ntile_max)
    tabs_dn = _step_tables(ntile, tstart, w_down.shape[-1] // tn_dn, ntile_max)
    act = _moe_mm(xs, w_gate_up[0], b_gate_up[0], tabs_gu, swiglu=True, tm=tmm, tn=tn_gu)
    ys = _moe_mm(act, w_down[0], b_down[0], tabs_dn, swiglu=False, tm=tmm, tn=tn_dn)
    y = _moe_combine(x1, gates, dest, ys, final_norm_g, 32)

    y_prompt = y[:rows].reshape(nb, seq, d)
    y_sample = y[rows:].reshape(ns, 1, d)
    kq = 2 * nh
    dv = aw // nh
    new_k_prompt = jnp.concatenate([jnp.broadcast_to(k_meta.reshape(1, N_META, kq, dq), (nb, N_META, kq, dq)),
                                    k_f.reshape(nb, seq, kq, dq)], axis=1)[None]
    new_v_prompt = jnp.concatenate([jnp.broadcast_to(v_meta.reshape(1, N_META, nh, dv), (nb, N_META, nh, dv)),
                                    v_f.reshape(nb, seq, nh, dv)], axis=1)[None]
    new_conv_prompt = u_p.reshape(nb, seq, cw)[:, seq - (CONV_K - 1):][None]
    new_k_sample = k_smp.reshape(1, ns, 1, kq, dq)
    new_v_sample = v_smp.reshape(1, ns, 1, nh, dv)
    new_conv_sample = jnp.concatenate([state_conv[0][:, 1:], u_smp[:, None, :]], axis=1)[None]
    return (y_prompt, y_sample, new_k_prompt, new_v_prompt, new_conv_prompt,
            new_k_sample, new_v_sample, new_conv_sample)
```

```python
import functools
import math

import numpy as np
import jax
import jax.numpy as jnp
from jax import lax
from jax.experimental import pallas as pl
from jax.experimental.pallas import tpu as pltpu

F32 = jnp.float32
BF16 = jnp.bfloat16

N_META = 16
N_HEADS = 16
CONV_K = 31
TOP_K = 4
SWIGLU_LIMIT = 7.0
SWIGLU_ALPHA = 1.702
ROPE_THETA = 10000.0
NORM_EPS = 1e-6
LN_EPS = 1e-5
LANES = 128
VMEM_LIMIT = 56 * 1024 * 1024


def _cparams(*sem):
    return pltpu.CompilerParams(dimension_semantics=sem, vmem_limit_bytes=VMEM_LIMIT)


def _dot(a, b):
    return jnp.dot(a, b, preferred_element_type=F32)


def _dot_nt(a, b):
    return lax.dot_general(a, b, (((1,), (1,)), ((), ())), preferred_element_type=F32)


def _split(x):
    hi = x.astype(BF16)
    lo = (x - hi.astype(F32)).astype(BF16)
    return hi, lo


def _rms(x, g, eps):
    return x * lax.rsqrt(jnp.mean(x * x, axis=-1, keepdims=True) + eps) * g


def _rmsnorm_body(x_ref, g_ref, o_ref, *, eps):
    o_ref[...] = _rms(x_ref[...], g_ref[...], eps).astype(o_ref.dtype)


def _rmsnorm(x, g, out_dtype, tm):
    m, d = x.shape
    return pl.pallas_call(
        functools.partial(_rmsnorm_body, eps=NORM_EPS),
        grid=(pl.cdiv(m, tm),),
        in_specs=[pl.BlockSpec((tm, d), lambda i: (i, 0)), pl.BlockSpec((1, d), lambda i: (0, 0))],
        out_specs=pl.BlockSpec((tm, d), lambda i: (i, 0)),
        out_shape=jax.ShapeDtypeStruct((m, d), out_dtype),
        compiler_params=_cparams("parallel"),
        name="rmsnorm",
    )(x, g.reshape(1, d))


def _mm(x_refs, w_ref):
    z = None
    r0 = 0
    for xr in x_refs:
        kk = xr.shape[1]
        zp = _dot(xr[...], w_ref[r0:r0 + kk, :])
        z = zp if z is None else z + zp
        r0 += kk
    return z


def _rope_body(x_ref, w_ref, c_ref, s_ref, *o_refs, scale, tn):
    z = _mm([x_ref], w_ref)
    c = c_ref[...]
    s = s_ref[...]
    lane = lax.broadcasted_iota(jnp.int32, (1, LANES), 1)
    first = (lane % 64) < 32
    for j in range(tn // LANES):
        zj = z[:, LANES * j:LANES * (j + 1)]
        rot = jnp.where(first, pltpu.roll(zj, LANES - 32, 1), pltpu.roll(zj, 32, 1))
        r = zj * c + rot * s
        if scale != 1.0:
            r = r * scale
        for o in o_refs:
            o[:, LANES * j:LANES * (j + 1)] = r.astype(o.dtype)


def _plain_body(x_ref, w_ref, *o_refs):
    z = _mm([x_ref], w_ref)
    for o in o_refs:
        o[...] = z.astype(o.dtype)


def _glu_body(x_ref, wa_ref, wg_ref, ba_ref, bg_ref, o_ref):
    a = _mm([x_ref], wa_ref) + ba_ref[...]
    g = _mm([x_ref], wg_ref) + bg_ref[...]
    o_ref[...] = a * jax.nn.sigmoid(g)


def _res_body(*refs, nparts):
    x_refs = refs[:nparts]
    w_ref, r_ref, o_ref = refs[nparts:]
    o_ref[...] = _mm(x_refs, w_ref) + r_ref[...]


def _proj_rope(x, w, col0, ncols, cs, sn, *, scale, out_dtypes, tm, tn, tab_blocks):
    m, k = x.shape
    cb = col0 // tn
    return pl.pallas_call(
        functools.partial(_rope_body, scale=scale, tn=tn),
        grid=(m // tm, ncols // tn),
        in_specs=[pl.BlockSpec((tm, k), lambda i, j: (i, 0)),
                  pl.BlockSpec((k, tn), lambda i, j: (0, cb + j)),
                  pl.BlockSpec((tm, LANES), lambda i, j: (i % tab_blocks, 0)),
                  pl.BlockSpec((tm, LANES), lambda i, j: (i % tab_blocks, 0))],
        out_specs=[pl.BlockSpec((tm, tn), lambda i, j: (i, j)) for _ in out_dtypes],
        out_shape=[jax.ShapeDtypeStruct((m, ncols), dt) for dt in out_dtypes],
        compiler_params=_cparams("parallel", "arbitrary"),
        name="proj_rope",
    )(x, w, cs, sn)


def _proj_plain(x, w, col0, ncols, *, out_dtypes, tm, tn):
    m, k = x.shape
    cb = col0 // tn
    return pl.pallas_call(
        _plain_body,
        grid=(m // tm, ncols // tn),
        in_specs=[pl.BlockSpec((tm, k), lambda i, j: (i, 0)),
                  pl.BlockSpec((k, tn), lambda i, j: (0, cb + j))],
        out_specs=[pl.BlockSpec((tm, tn), lambda i, j: (i, j)) for _ in out_dtypes],
        out_shape=[jax.ShapeDtypeStruct((m, ncols), dt) for dt in out_dtypes],
        compiler_params=_cparams("parallel", "arbitrary"),
        name="proj_plain",
    )(x, w)


def _proj_glu(x, w, col_a, col_g, ncols, bias, *, tm, tn):
    m, k = x.shape
    ca, cg = col_a // tn, col_g // tn
    nb = ncols // tn
    b2 = bias.reshape(1, 2 * ncols)
    return pl.pallas_call(
        _glu_body,
        grid=(m // tm, nb),
        in_specs=[pl.BlockSpec((tm, k), lambda i, j: (i, 0)),
                  pl.BlockSpec((k, tn), lambda i, j: (0, ca + j)),
                  pl.BlockSpec((k, tn), lambda i, j: (0, cg + j)),
                  pl.BlockSpec((1, tn), lambda i, j: (0, j)),
                  pl.BlockSpec((1, tn), lambda i, j: (0, nb + j))],
        out_specs=pl.BlockSpec((tm, tn), lambda i, j: (i, j)),
        out_shape=jax.ShapeDtypeStruct((m, ncols), F32),
        compiler_params=_cparams("parallel", "arbitrary"),
        name="proj_glu",
    )(x, w, w, b2, b2)


def _proj_res(xs, w, res, *, tm, tn):
    m = xs[0].shape[0]
    k, n = w.shape
    return pl.pallas_call(
        functools.partial(_res_body, nparts=len(xs)),
        grid=(m // tm, n // tn),
        in_specs=[pl.BlockSpec((tm, x.shape[1]), lambda i, j: (i, 0)) for x in xs]
                 + [pl.BlockSpec((k, tn), lambda i, j: (0, j)),
                    pl.BlockSpec((tm, tn), lambda i, j: (i, j))],
        out_specs=pl.BlockSpec((tm, tn), lambda i, j: (i, j)),
        out_shape=jax.ShapeDtypeStruct((m, n), F32),
        compiler_params=_cparams("parallel", "arbitrary"),
        name="proj_res",
    )(*xs, w, res)


def _rope_tables(pos, d):
    inv = ROPE_THETA ** (-jnp.arange(0, d, 2, dtype=F32) / d)
    ang = pos.astype(F32)[:, None] * inv[None, :]
    cos, sin = jnp.cos(ang), jnp.sin(ang)
    reps = LANES // d
    return (jnp.concatenate([cos, cos] * reps, axis=-1),
            jnp.concatenate([-sin, sin] * reps, axis=-1))


def _diff_lambda(lq1, lk1, lq2, lk2, lam_init):
    return (jnp.exp(jnp.sum(lq1[...] * lk1[...], keepdims=True))
            - jnp.exp(jnp.sum(lq2[...] * lk2[...], keepdims=True)) + lam_init)


def _attn_body(lq1, lk1, lq2, lk2, g_ref, q_ref, k_ref, v_ref, km_ref, vm_ref, o_ref,
               m_ref, l_ref, acc_ref, *, tq, n_meta, lam_init, nchunk):
    qi = pl.program_id(2)
    q = q_ref[0]
    lane = lax.broadcasted_iota(jnp.int32, (1, LANES), 1)
    zq = jnp.zeros_like(q)
    qq = jnp.concatenate([jnp.where(lane < 64, q, zq), jnp.where(lane >= 64, q, zq)], axis=0)

    rc = 2 * tq // nchunk
    chunks = [(c, pl.ds(c * rc, rc), qq[c * rc:(c + 1) * rc]) for c in range(nchunk)]

    for _, rows, qc in chunks:
        s = jnp.where(lane < n_meta, _dot_nt(qc, km_ref[...]), -jnp.inf)
        m0 = jnp.max(s, axis=1, keepdims=True)
        p0 = jnp.exp(s - m0)
        m_ref[rows, :] = m0
        l_ref[rows, :] = jnp.sum(p0, axis=1, keepdims=True)
        acc_ref[rows, :] = _dot(p0.astype(BF16), vm_ref[...])

    def update(rows, s, v):
        m_prev = m_ref[rows, :]
        m_new = jnp.maximum(m_prev, jnp.max(s, axis=1, keepdims=True))
        alpha = jnp.exp(m_prev - m_new)
        p = jnp.exp(s - m_new)
        l_ref[rows, :] = alpha * l_ref[rows, :] + jnp.sum(p, axis=1, keepdims=True)
        acc_ref[rows, :] = alpha * acc_ref[rows, :] + _dot(p.astype(BF16), v)
        m_ref[rows, :] = m_new

    def body(kb, carry):
        off = pl.multiple_of(kb * tq, tq)
        k, v = k_ref[0, pl.ds(off, tq), :], v_ref[0, pl.ds(off, tq), :]
        for _, rows, qc in chunks:
            update(rows, _dot_nt(qc, k), v)
        return carry

    lax.fori_loop(0, qi, body, 0)

    off = pl.multiple_of(qi * tq, tq)
    k, v = k_ref[0, pl.ds(off, tq), :], v_ref[0, pl.ds(off, tq), :]
    col = lax.broadcasted_iota(jnp.int32, (rc, tq), 1)
    for c, rows, qc in chunks:
        row = lax.broadcasted_iota(jnp.int32, (rc, tq), 0) + (c * rc) % tq
        update(rows, jnp.where(col <= row, _dot_nt(qc, k), -jnp.inf), v)

    lam = _diff_lambda(lq1, lk1, lq2, lk2, lam_init)
    o = acc_ref[...] / l_ref[...]
    o = o[:tq] - lam * o[tq:]
    o = _rms(o, g_ref[...], NORM_EPS) * (1.0 - lam_init)
    o_ref[0] = o.astype(o_ref.dtype)


def _attn_prompt(q, k, v, km, vm, lams, sub_g, lam_init, tq):
    b, s, w = q.shape
    nh = w // LANES
    vec = lambda a: a.reshape(1, -1)
    small = pl.BlockSpec((1, lams[0].shape[-1]), lambda bi, h, i: (0, 0))
    return pl.pallas_call(
        functools.partial(_attn_body, tq=tq, n_meta=N_META, lam_init=lam_init, nchunk=4),
        grid=(b, nh, s // tq),
        in_specs=[small, small, small, small,
                  pl.BlockSpec((1, LANES), lambda bi, h, i: (0, 0)),
                  pl.BlockSpec((1, tq, LANES), lambda bi, h, i: (bi, i, h)),
                  pl.BlockSpec((1, s, LANES), lambda bi, h, i: (bi, 0, h)),
                  pl.BlockSpec((1, s, LANES), lambda bi, h, i: (bi, 0, h)),
                  pl.BlockSpec((LANES, LANES), lambda bi, h, i: (0, h)),
                  pl.BlockSpec((LANES, LANES), lambda bi, h, i: (0, h))],
        out_specs=pl.BlockSpec((1, tq, LANES), lambda bi, h, i: (bi, i, h)),
        out_shape=jax.ShapeDtypeStruct((b, s, w), BF16),
        scratch_shapes=[pltpu.VMEM((2 * tq, 1), F32), pltpu.VMEM((2 * tq, 1), F32),
                        pltpu.VMEM((2 * tq, LANES), F32)],
        compiler_params=_cparams("parallel", "parallel", "arbitrary"),
        name="attn_prompt",
    )(*[vec(a) for a in lams], vec(sub_g), q, k, v, km, vm)


def _ln_silu(y, g, b):
    mu = jnp.mean(y, axis=-1, keepdims=True)
    d = y - mu
    var = jnp.mean(d * d, axis=-1, keepdims=True)
    z = d * lax.rsqrt(var + LN_EPS) * g + b
    return z * jax.nn.sigmoid(z)


def _conv_body(head_ref, prev_ref, cur_ref, w_ref, b_ref, g_ref, bb_ref, o_ref, ext_ref, y_ref, *, tr, kc, halo):
    i = pl.program_id(1)

    @pl.when(i == 0)
    def _():
        ext_ref[0:halo, :] = head_ref[...]

    @pl.when(i > 0)
    def _():
        ext_ref[0:halo, :] = prev_ref[...]

    ext_ref[halo:halo + tr, :] = cur_ref[...]
    c = cur_ref.shape[1]
    rc, cc = 64, 256
    off0 = halo - (kc - 1)
    for r0 in range(0, tr, rc):
        for c0 in range(0, c, cc):
            acc = jnp.zeros((rc, cc), F32)
            for j in range(kc):
                acc = acc + ext_ref[r0 + off0 + j:r0 + off0 + j + rc, c0:c0 + cc] * w_ref[j:j + 1, c0:c0 + cc]
            y_ref[r0:r0 + rc, c0:c0 + cc] = acc + b_ref[:, c0:c0 + cc]
    o_ref[...] = _ln_silu(y_ref[...], g_ref[...], bb_ref[...]).astype(o_ref.dtype)


def _conv_prompt(u, head, dw_w, dw_b, ln_g, ln_b, nb, tr):
    rows, c = u.shape
    s = rows // nb
    halo = head.shape[0]
    kc = dw_w.shape[0]
    vec = lambda a: a.reshape(1, c)
    cst = lambda bi, i: (0, 0)
    return pl.pallas_call(
        functools.partial(_conv_body, tr=tr, kc=kc, halo=halo),
        grid=(nb, s // tr),
        in_specs=[pl.BlockSpec((halo, c), cst),
                  pl.BlockSpec((halo, c), lambda bi, i: (jnp.maximum((bi * s + i * tr) // halo - 1, 0), 0)),
                  pl.BlockSpec((tr, c), lambda bi, i: (bi * (s // tr) + i, 0)),
                  pl.BlockSpec((kc, c), cst), pl.BlockSpec((1, c), cst),
                  pl.BlockSpec((1, c), cst), pl.BlockSpec((1, c), cst)],
        out_specs=pl.BlockSpec((tr, c), lambda bi, i: (bi * (s // tr) + i, 0)),
        out_shape=jax.ShapeDtypeStruct((rows, c), BF16),
        scratch_shapes=[pltpu.VMEM((halo + tr, c), F32), pltpu.VMEM((tr, c), F32)],
        compiler_params=_cparams("parallel", "arbitrary"),
        name="conv_prompt",
    )(head, u, u, dw_w, vec(dw_b), vec(ln_g), vec(ln_b))


def _sconv_body(st_ref, u_ref, w_ref, b_ref, g_ref, bb_ref, o_ref, *, kc):
    w = w_ref[...]
    y = jnp.sum(st_ref[...] * w[None, :kc - 1, :], axis=1) + u_ref[...] * w[kc - 1:kc, :] + b_ref[...]
    o_ref[...] = _ln_silu(y, g_ref[...], bb_ref[...])


def _conv_sample(state, u, dw_w, dw_b, ln_g, ln_b):
    nb, _, c = state.shape
    vec = lambda a: a.reshape(1, c)
    return pl.pallas_call(
        functools.partial(_sconv_body, kc=dw_w.shape[0]),
        out_shape=jax.ShapeDtypeStruct((nb, c), F32),
        compiler_params=pltpu.CompilerParams(vmem_limit_bytes=VMEM_LIMIT),
        name="conv_sample",
    )(state, u, dw_w, vec(dw_b), vec(ln_g), vec(ln_b))


def _bf(x):
    return x.astype(BF16).astype(F32)


def _sattn_body(pt_ref, lq1, lk1, lq2, lk2, g_ref, e_ref, q_ref, qb_ref, kn_ref, vn_ref, kc_ref, vc_ref, o_ref,
                s_ref, a_ref, acc_ref, anew_ref, *, nh, npg, page, dv, lam_init):
    p = pl.program_id(1)
    nsub = 2 * nh
    col = lax.broadcasted_iota(jnp.int32, (nsub, nsub), 1)
    own = lax.broadcasted_iota(jnp.int32, (nsub, nsub), 0) == 2 * (col % nh) + col // nh

    @pl.when(p < npg)
    def _():
        k2 = kc_ref[0, 0].reshape(page * nsub, kc_ref.shape[-1]).astype(BF16)
        r = _dot(k2, qb_ref[0]).reshape(page, nsub, nsub)
        off = pl.multiple_of(p * page, page)
        s_ref[pl.ds(off, page), :] = jnp.sum(jnp.where(own[None], r, 0.0), axis=1)

    @pl.when(p == npg - 1)
    def _():
        lam = _diff_lambda(lq1, lk1, lq2, lk2, lam_init)
        sn = jnp.sum(kn_ref[0] * q_ref[0], axis=-1, keepdims=True)
        sn = jnp.sum(jnp.where(own, sn, 0.0), axis=0, keepdims=True)
        s = s_ref[...]
        m = jnp.maximum(jnp.max(s, axis=0, keepdims=True), sn)
        e, en = jnp.exp(s - m), jnp.exp(sn - m)
        l = jnp.sum(e, axis=0, keepdims=True) + en
        pn, pnn = e / l, en / l
        a_ref[...] = pn[:, :nh] - lam * pn[:, nh:]
        anew_ref[...] = pnn[:, :nh] - lam * pnn[:, nh:]
        acc_ref[...] = jnp.zeros(acc_ref.shape, F32)

    @pl.when(p >= npg)
    def _():
        off = pl.multiple_of((p - npg) * page, page)
        a = a_ref[pl.ds(off, page), :].astype(BF16)
        ae = _dot(a, e_ref[...])
        vd = jnp.concatenate([vc_ref[0, 0, :, h, :] for h in range(nh)], axis=1)
        prod = ae * _bf(vd)
        acc_ref[...] += prod.reshape(page // 8, 8, nh * dv).sum(axis=0)

    @pl.when(p == 2 * npg - 1)
    def _():
        hi, lo = _split(jnp.broadcast_to(anew_ref[...], (16, nh)))
        ane = (_dot(hi, e_ref[...]) + _dot(lo, e_ref[...]))[0:1]
        o = jnp.sum(acc_ref[...], axis=0, keepdims=True) + ane * vn_ref[0]
        for h in range(nh):
            o_ref[0, :, h * dv:(h + 1) * dv] = (_rms(o[:, h * dv:(h + 1) * dv], g_ref[...], NORM_EPS)
                                                * (1.0 - lam_init))


def _attn_sample(q, kn, vn, cache_k, cache_v, page_table, lams, sub_g, lam_init):
    nb, nsub, dq = q.shape
    nh = nsub // 2
    dv = vn.shape[-1] // nh
    npg = page_table.shape[1]
    page = cache_k.shape[2]
    order = np.concatenate([np.arange(0, nsub, 2), np.arange(1, nsub, 2)])
    qblk = jnp.swapaxes(q[:, order, :], 1, 2).astype(BF16)
    expand = jnp.asarray(np.kron(np.eye(nh, dtype=np.float32), np.ones((1, dv), np.float32)), BF16)
    vec = lambda a: a.reshape(1, -1)
    small = pl.BlockSpec((1, lams[0].shape[-1]), lambda b, p, pt: (0, 0))
    grid_spec = pltpu.PrefetchScalarGridSpec(
        num_scalar_prefetch=1,
        grid=(nb, 2 * npg),
        in_specs=[small, small, small, small,
                  pl.BlockSpec((1, dv), lambda b, p, pt: (0, 0)),
                  pl.BlockSpec((nh, nh * dv), lambda b, p, pt: (0, 0)),
                  pl.BlockSpec((1, nsub, dq), lambda b, p, pt: (b, 0, 0)),
                  pl.BlockSpec((1, dq, nsub), lambda b, p, pt: (b, 0, 0)),
                  pl.BlockSpec((1, nsub, dq), lambda b, p, pt: (b, 0, 0)),
                  pl.BlockSpec((1, 1, nh * dv), lambda b, p, pt: (b, 0, 0)),
                  pl.BlockSpec((1, 1, page, nsub, dq),
                               lambda b, p, pt: (0, pt[b * npg + jnp.minimum(p, npg - 1)], 0, 0, 0)),
                  pl.BlockSpec((1, 1, page, nh, dv),
                               lambda b, p, pt: (0, pt[b * npg + jnp.maximum(p - npg, 0)], 0, 0, 0))],
        out_specs=pl.BlockSpec((1, 1, nh * dv), lambda b, p, pt: (b, 0, 0)),
        scratch_shapes=[pltpu.VMEM((npg * page, nsub), F32), pltpu.VMEM((npg * page, nh), F32),
                        pltpu.VMEM((8, nh * dv), F32), pltpu.VMEM((1, nh), F32)],
    )
    return pl.pallas_call(
        functools.partial(_sattn_body, nh=nh, npg=npg, page=page, dv=dv, lam_init=lam_init),
        grid_spec=grid_spec,
        out_shape=jax.ShapeDtypeStruct((nb, 1, nh * dv), F32),
        compiler_params=_cparams("parallel", "arbitrary"),
        name="attn_sample",
    )(page_table.reshape(-1), *[vec(a) for a in lams], vec(sub_g), expand, q, qblk, kn,
      vn.reshape(nb, 1, nh * dv), cache_k, cache_v)


def _router_body(x_ref, g_ref, w_ref, b_ref, gate_ref, idx_ref, *, topk):
    tm = x_ref.shape[0]
    h = _rms(x_ref[...], g_ref[...], NORM_EPS)
    logits = _dot(h.astype(BF16), w_ref[...].astype(BF16)) + b_ref[...]
    ne = logits.shape[1]
    eidx = lax.broadcasted_iota(jnp.int32, logits.shape, 1)
    vals, idxs = [], []
    cur = logits
    for _ in range(topk):
        mx = jnp.max(cur, axis=-1, keepdims=True)
        ix = jnp.min(jnp.where(cur == mx, eidx, ne), axis=-1, keepdims=True)
        vals.append(mx)
        idxs.append(ix)
        cur = jnp.where(eidx == ix, -jnp.inf, cur)
    ex = [jnp.exp(v - vals[0]) for v in vals]
    tot = ex[0]
    for e in ex[1:]:
        tot = tot + e
    lane = lax.broadcasted_iota(jnp.int32, (tm, LANES), 1)
    gates = jnp.zeros((tm, LANES), F32)
    ids = jnp.zeros((tm, LANES), jnp.int32)
    for kk in range(topk):
        gates = jnp.where(lane == kk, ex[kk] / tot, gates)
        ids = jnp.where(lane == kk, idxs[kk], ids)
    gate_ref[...] = gates
    idx_ref[...] = ids


def _router(x, g, w, b, tm):
    t, d = x.shape
    ne = w.shape[1]
    return pl.pallas_call(
        functools.partial(_router_body, topk=TOP_K),
        grid=(pl.cdiv(t, tm),),
        in_specs=[pl.BlockSpec((tm, d), lambda i: (i, 0)), pl.BlockSpec((1, d), lambda i: (0, 0)),
                  pl.BlockSpec((d, ne), lambda i: (0, 0)), pl.BlockSpec((1, ne), lambda i: (0, 0))],
        out_specs=[pl.BlockSpec((tm, LANES), lambda i: (i, 0)), pl.BlockSpec((tm, LANES), lambda i: (i, 0))],
        out_shape=[jax.ShapeDtypeStruct((t, LANES), F32), jax.ShapeDtypeStruct((t, LANES), jnp.int32)],
        compiler_params=_cparams("parallel"),
        name="router",
    )(x, g.reshape(1, d), w, b.reshape(1, ne))


def _gather_body(src_ref, nact_ref, x_hbm, g_ref, o_ref, buf, sem, *, tm):
    i = pl.program_id(0)

    @pl.when(i < nact_ref[0])
    def _():
        def issue(r, c):
            tok = src_ref[i * tm + r]
            pltpu.make_async_copy(x_hbm.at[pl.ds(tok, 1)], buf.at[pl.ds(r, 1)], sem).start()
            return c

        lax.fori_loop(0, tm, issue, 0)
        pltpu.make_async_copy(x_hbm.at[pl.ds(0, tm)], buf, sem).wait()
        o_ref[...] = _rms(buf[...], g_ref[...], NORM_EPS).astype(o_ref.dtype)

    @pl.when(i >= nact_ref[0])
    def _():
        o_ref[...] = jnp.zeros(o_ref.shape, o_ref.dtype)


def _moe_gather(x, g, src_tok, n_tiles, ntile_max, tm):
    d = x.shape[1]
    grid_spec = pltpu.PrefetchScalarGridSpec(
        num_scalar_prefetch=2,
        grid=(ntile_max,),
        in_specs=[pl.BlockSpec(memory_space=pl.ANY), pl.BlockSpec((1, d), lambda i, s, n: (0, 0))],
        out_specs=pl.BlockSpec((tm, d), lambda i, s, n: (i, 0)),
        scratch_shapes=[pltpu.VMEM((tm, d), F32), pltpu.SemaphoreType.DMA(())],
    )
    return pl.pallas_call(
        functools.partial(_gather_body, tm=tm),
        grid_spec=grid_spec,
        out_shape=jax.ShapeDtypeStruct((ntile_max * tm, d), BF16),
        compiler_params=_cparams("arbitrary"),
        name="moe_gather",
    )(src_tok, n_tiles.reshape(1), x, g.reshape(1, d))


def _deinterleave_matrix(n):
    p = np.zeros((n, n), np.float32)
    half = n // 2
    p[2 * np.arange(half), np.arange(half)] = 1.0
    p[2 * np.arange(half) + 1, half + np.arange(half)] = 1.0
    return jnp.asarray(p, BF16)


def _moe_mm_body(e_ref, c_ref, t_ref, f_ref, nact_ref, x_ref, w_ref, b_ref, *rest, swiglu, pw):
    if swiglu:
        p_ref, o_ref, wbf = rest
    else:
        o_ref, wbf = rest
    i = pl.program_id(0)

    @pl.when(i < nact_ref[0])
    def _():
        @pl.when(f_ref[i] == 1)
        def _():
            wbf[...] = w_ref[0].astype(BF16)

        z = _dot(x_ref[...], wbf[...]) + b_ref[0]
        if not swiglu:
            o_ref[...] = z
            return
        half = pw // 2
        for j in range(z.shape[1] // pw):
            hi, lo = _split(z[:, pw * j:pw * (j + 1)])
            d = _dot(hi, p_ref[...]) + _dot(lo, p_ref[...])
            gate = jnp.minimum(d[:, :half], SWIGLU_LIMIT)
            up = jnp.clip(d[:, half:], -SWIGLU_LIMIT, SWIGLU_LIMIT)
            act = (up + 1.0) * (gate * jax.nn.sigmoid(SWIGLU_ALPHA * gate))
            o_ref[:, half * j:half * (j + 1)] = act.astype(o_ref.dtype)

    @pl.when(i >= nact_ref[0])
    def _():
        o_ref[...] = jnp.zeros(o_ref.shape, o_ref.dtype)


def _moe_mm(x, w, b, tabs, *, swiglu, tm, tn):
    e_tab, c_tab, t_tab, f_tab, nact = tabs
    ne, k, n = w.shape
    nsteps = e_tab.shape[0]
    rows = x.shape[0]
    n_out = n // 2 if swiglu else n
    tn_out = tn // 2 if swiglu else tn
    pw = 2 * LANES
    in_specs = [pl.BlockSpec((tm, k), lambda i, e, c, t, f, a: (t[i], 0)),
                pl.BlockSpec((1, k, tn), lambda i, e, c, t, f, a: (e[i], 0, c[i])),
                pl.BlockSpec((1, 1, tn), lambda i, e, c, t, f, a: (e[i], 0, c[i]))]
    args = [x, w, b.reshape(ne, 1, n)]
    if swiglu:
        in_specs.append(pl.BlockSpec((pw, pw), lambda i, e, c, t, f, a: (0, 0)))
        args.append(_deinterleave_matrix(pw))
    grid_spec = pltpu.PrefetchScalarGridSpec(
        num_scalar_prefetch=5,
        grid=(nsteps,),
        in_specs=in_specs,
        out_specs=pl.BlockSpec((tm, tn_out), lambda i, e, c, t, f, a: (t[i], c[i])),
        scratch_shapes=[pltpu.VMEM((k, tn), BF16)],
    )
    return pl.pallas_call(
        functools.partial(_moe_mm_body, swiglu=swiglu, pw=pw),
        grid_spec=grid_spec,
        out_shape=jax.ShapeDtypeStruct((rows, n_out), BF16 if swiglu else F32),
        compiler_params=_cparams("arbitrary"),
        name="moe_gate_up" if swiglu else "moe_down",
    )(e_tab, c_tab, t_tab, f_tab, nact, *args)


def _combine_body(dest_ref, x_ref, gate_ref, ys_hbm, fg_ref, o_ref, buf, sem, *, tt, topk):
    i = pl.program_id(0)

    def issue(r, c):
        for kk in range(topk):
            d = dest_ref[(i * tt + r) * topk + kk]
            pltpu.make_async_copy(ys_hbm.at[pl.ds(d, 1)], buf.at[kk, pl.ds(r, 1)], sem).start()
        return c

    lax.fori_loop(0, tt, issue, 0)
    for kk in range(topk):
        pltpu.make_async_copy(ys_hbm.at[pl.ds(0, tt)], buf.at[kk], sem).wait()
    gates = gate_ref[...]
    y = x_ref[...]
    for kk in range(topk):
        y = y + gates[:, kk:kk + 1] * buf[kk]
    o_ref[...] = _rms(y, fg_ref[...], NORM_EPS)


def _moe_combine(x, gates, dest, ys, fg, tt):
    t, d = x.shape
    assert t % tt == 0
    grid_spec = pltpu.PrefetchScalarGridSpec(
        num_scalar_prefetch=1,
        grid=(t // tt,),
        in_specs=[pl.BlockSpec((tt, d), lambda i, s: (i, 0)),
                  pl.BlockSpec((tt, LANES), lambda i, s: (i, 0)),
                  pl.BlockSpec(memory_space=pl.ANY),
                  pl.BlockSpec((1, d), lambda i, s: (0, 0))],
        out_specs=pl.BlockSpec((tt, d), lambda i, s: (i, 0)),
        scratch_shapes=[pltpu.VMEM((TOP_K, tt, d), F32), pltpu.SemaphoreType.DMA(())],
    )
    return pl.pallas_call(
        functools.partial(_combine_body, tt=tt, topk=TOP_K),
        grid_spec=grid_spec,
        out_shape=jax.ShapeDtypeStruct((t, d), F32),
        compiler_params=_cparams("arbitrary"),
        name="moe_combine",
    )(dest, x, gates, ys, fg.reshape(1, d))


def _route_tables(top_i, ne, tm, ntile_max):
    t, topk = top_i.shape
    r = t * topk
    e_flat = top_i.reshape(-1)
    counts = jnp.zeros((ne,), jnp.int32).at[e_flat].add(1)
    ntile = (counts + tm - 1) // tm
    tend = jnp.cumsum(ntile)
    tstart = tend - ntile
    order = jnp.argsort(e_flat, stable=True).astype(jnp.int32)
    e_s = e_flat[order]
    start = jnp.cumsum(counts) - counts
    dest_sorted = tstart[e_s] * tm + jnp.arange(r, dtype=jnp.int32) - start[e_s]
    dest = jnp.zeros((r,), jnp.int32).at[order].set(dest_sorted)
    src_tok = jnp.zeros((ntile_max * tm,), jnp.int32).at[dest_sorted].set(order // topk)
    return dest, src_tok, ntile, tstart, tend[-1]


def _step_tables(ntile, tstart, nchunk, ntile_max):
    per = ntile * nchunk
    send = jnp.cumsum(per)
    sstart = send - per
    nact = send[-1]
    n_tiles = nact // nchunk
    i = jnp.arange(ntile_max * nchunk, dtype=jnp.int32)
    active = i < nact
    ic = jnp.minimum(i, nact - 1)
    e = jnp.searchsorted(send, ic, side="right").astype(jnp.int32)
    local = ic - sstart[e]
    nt = jnp.maximum(ntile[e], 1)
    tail = i - nact
    c = jnp.where(active, local // nt, tail % nchunk)
    tile = jnp.where(active, tstart[e] + local % nt, n_tiles + tail // nchunk)
    first = ((local % nt == 0) & active).astype(jnp.int32)
    return (e, c.astype(jnp.int32), tile.astype(jnp.int32), first, nact.reshape(1).astype(jnp.int32))


def kernel(x_prompt, x_sample, cache_k, cache_v, state_conv, page_table, meta_tokens, norm_mix_g, w_in, b_conv_in,
           lambda_q1, lambda_k1, lambda_q2, lambda_k2, subln_g, dw_w, dw_b, conv_ln_g, conv_ln_b, w_out, norm_ffn_g,
           router_w, router_b, w_gate_up, b_gate_up, w_down, b_down, final_norm_g):
    nb, seq, d = x_prompt.shape
    ns = x_sample.shape[0]
    depth = w_in.shape[0]
    assert depth == 1 and x_sample.shape[1] == 1
    aw = cache_v.shape[-1] * cache_v.shape[-2]
    cw = state_conv.shape[-1]
    nh = cache_v.shape[-2]
    dq = cache_k.shape[-1]
    ne = router_w.shape[-1]
    lam_init = 0.8 - 0.6 * math.exp(-0.3 * 0)
    scale = dq ** -0.5
    lams = (lambda_q1[0], lambda_k1[0], lambda_q2[0], lambda_k2[0])
    w_in0, w_out0 = w_in[0], w_out[0]
    rows = nb * seq
    nsm = N_META + ns

    tm = min(1024, seq)
    tn = 512
    tq = min(512, seq)

    w_in_b = w_in0.astype(BF16)
    xn = _rmsnorm(x_prompt.reshape(rows, d), norm_mix_g[0], BF16, 256)
    cos_p, sin_p = _rope_tables(N_META + jnp.arange(seq), dq)
    (q_b,) = _proj_rope(xn, w_in_b, 0, aw, cos_p, sin_p, scale=scale, out_dtypes=(BF16,),
                        tm=tm, tn=tn, tab_blocks=seq // tm)
    k_f, k_b = _proj_rope(xn, w_in_b, aw, aw, cos_p, sin_p, scale=1.0, out_dtypes=(F32, BF16),
                          tm=tm, tn=tn, tab_blocks=seq // tm)
    v_f, v_b = _proj_plain(xn, w_in_b, 2 * aw, aw, out_dtypes=(F32, BF16), tm=tm, tn=tn)
    u_p = _proj_glu(xn, w_in_b, 3 * aw, 3 * aw + cw, cw, b_conv_in[0], tm=tm, tn=tn)

    x_small = jnp.concatenate([meta_tokens, x_sample[:, 0, :]], axis=0)
    xn_s = _rmsnorm(x_small, norm_mix_g[0], BF16, nsm)
    pos_s = jnp.concatenate([jnp.arange(N_META), jnp.full((ns,), page_table.shape[1] * cache_k.shape[2])])
    cos_s, sin_s = _rope_tables(pos_s, dq)
    (q_s,) = _proj_rope(xn_s, w_in_b, 0, aw, cos_s, sin_s, scale=scale, out_dtypes=(F32,),
                        tm=nsm, tn=tn, tab_blocks=1)
    (k_s,) = _proj_rope(xn_s, w_in_b, aw, aw, cos_s, sin_s, scale=1.0, out_dtypes=(F32,),
                        tm=nsm, tn=tn, tab_blocks=1)
    (v_s,) = _proj_plain(xn_s, w_in_b, 2 * aw, aw, out_dtypes=(F32,), tm=nsm, tn=tn)
    u_s = _proj_glu(xn_s, w_in_b, 3 * aw, 3 * aw + cw, cw, b_conv_in[0], tm=nsm, tn=tn)
    k_meta, v_meta, u_meta = k_s[:N_META], v_s[:N_META], u_s[:N_META]
    q_smp, k_smp, v_smp, u_smp = q_s[N_META:], k_s[N_META:], v_s[N_META:], u_s[N_META:]

    pad = ((0, LANES - N_META), (0, 0))
    km = jnp.pad(k_meta, pad).astype(BF16)
    vm = jnp.pad(v_meta, pad).astype(BF16)
    o_p = _attn_prompt(q_b.reshape(nb, seq, aw), k_b.reshape(nb, seq, aw), v_b.reshape(nb, seq, aw),
                       km, vm, lams, subln_g[0], lam_init, tq)

    halo = 32
    head = jnp.concatenate([jnp.zeros((halo - N_META, cw), F32), u_meta], axis=0)
    yc_p = _conv_prompt(u_p, head, dw_w[0], dw_b[0], conv_ln_g[0], conv_ln_b[0], nb, 128)

    w_out_b = w_out0.astype(BF16)
    x1_p = _proj_res([o_p.reshape(rows, aw), yc_p], w_out_b, x_prompt.reshape(rows, d), tm=tm, tn=tn)

    o_s = _attn_sample(q_smp.reshape(ns, 2 * nh, dq), k_smp.reshape(ns, 2 * nh, dq), v_smp,
                       cache_k, cache_v, page_table, lams, subln_g[0], lam_init)
    yc_s = _conv_sample(state_conv[0], u_smp, dw_w[0], dw_b[0], conv_ln_g[0], conv_ln_b[0])
    x1_s = _proj_res([jnp.concatenate([o_s.reshape(ns, aw), yc_s], axis=1).astype(BF16)], w_out_b,
                     x_sample[:, 0, :], tm=ns, tn=tn)

    x1 = jnp.concatenate([x1_p, x1_s], axis=0)
    t_all = rows + ns
    gates, top_i = _router(x1, norm_ffn_g[0], router_w[0], router_b[0], 256)
    tmm = 512
    ntile_max = -(-(t_all * TOP_K + ne * (tmm - 1)) // tmm)
    dest, src_tok, ntile, tstart, n_tiles = _route_tables(top_i[:, :TOP_K], ne, tmm, ntile_max)
    xs = _moe_gather(x1, norm_ffn_g[0], src_tok, n_tiles, ntile_max, tmm)
    tn_gu = 512
    tn_dn = 512
    tabs_gu = _step_tables(ntile, tstart, w_gate_up.shape[-1] // tn_gu, ntile_max)
    tabs_dn = _step_tables(ntile, tstart, w_down.shape[-1] // tn_dn, ntile_max)
    act = _moe_mm(xs, w_gate_up[0], b_gate_up[0], tabs_gu, swiglu=True, tm=tmm, tn=tn_gu)
    ys = _moe_mm(act, w_down[0], b_down[0], tabs_dn, swiglu=False, tm=tmm, tn=tn_dn)
    y = _moe_combine(x1, gates, dest, ys, final_norm_g, 32)

    y_prompt = y[:rows].reshape(nb, seq, d)
    y_sample = y[rows:].reshape(ns, 1, d)
    kq = 2 * nh
    dv = aw // nh
    new_k_prompt = jnp.concatenate([jnp.broadcast_to(k_meta.reshape(1, N_META, kq, dq), (nb, N_META, kq, dq)),
                                    k_f.reshape(nb, seq, kq, dq)], axis=1)[None]
    new_v_prompt = jnp.concatenate([jnp.broadcast_to(v_meta.reshape(1, N_META, nh, dv), (nb, N_META, nh, dv)),
                                    v_f.reshape(nb, seq, nh, dv)], axis=1)[None]
    new_conv_prompt = u_p.reshape(nb, seq, cw)[:, seq - (CONV_K - 1):][None]
    new_k_sample = k_smp.reshape(1, ns, 1, kq, dq)
    new_v_sample = v_smp.reshape(1, ns, 1, nh, dv)
    new_conv_sample = jnp.concatenate([state_conv[0][:, 1:], u_smp[:, None, :]], axis=1)[None]
    return (y_prompt, y_sample, new_k_prompt, new_v_prompt, new_conv_prompt,
            new_k_sample, new_v_sample, new_conv_sample)
```

```python
import functools
import math

import numpy as np
import jax
import jax.numpy as jnp
from jax import lax
from jax.experimental import pallas as pl
from jax.experimental.pallas import tpu as pltpu

F32 = jnp.float32
BF16 = jnp.bfloat16

N_META = 16
N_HEADS = 16
CONV_K = 31
TOP_K = 4
SWIGLU_LIMIT = 7.0
SWIGLU_ALPHA = 1.702
ROPE_THETA = 10000.0
NORM_EPS = 1e-6
LN_EPS = 1e-5
LANES = 128
VMEM_LIMIT = 56 * 1024 * 1024


def _cparams(*sem):
    return pltpu.CompilerParams(dimension_semantics=sem, vmem_limit_bytes=VMEM_LIMIT)


def _dot(a, b):
    return jnp.dot(a, b, preferred_element_type=F32)


def _dot_nt(a, b):
    return lax.dot_general(a, b, (((1,), (1,)), ((), ())), preferred_element_type=F32)


def _split(x):
    hi = x.astype(BF16)
    lo = (x - hi.astype(F32)).astype(BF16)
    return hi, lo


def _rms(x, g, eps):
    return x * lax.rsqrt(jnp.mean(x * x, axis=-1, keepdims=True) + eps) * g


def _rmsnorm_body(x_ref, g_ref, o_ref, *, eps):
    o_ref[...] = _rms(x_ref[...], g_ref[...], eps).astype(o_ref.dtype)


def _rmsnorm(x, g, out_dtype, tm):
    m, d = x.shape
    return pl.pallas_call(
        functools.partial(_rmsnorm_body, eps=NORM_EPS),
        grid=(pl.cdiv(m, tm),),
        in_specs=[pl.BlockSpec((tm, d), lambda i: (i, 0)), pl.BlockSpec((1, d), lambda i: (0, 0))],
        out_specs=pl.BlockSpec((tm, d), lambda i: (i, 0)),
        out_shape=jax.ShapeDtypeStruct((m, d), out_dtype),
        compiler_params=_cparams("parallel"),
        name="rmsnorm",
    )(x, g.reshape(1, d))


def _mm(x_refs, w_ref):
    z = None
    r0 = 0
    for xr in x_refs:
        kk = xr.shape[1]
        zp = _dot(xr[...], w_ref[r0:r0 + kk, :])
        z = zp if z is None else z + zp
        r0 += kk
    return z


def _rope_body(x_ref, w_ref, c_ref, s_ref, *o_refs, scale, tn):
    z = _mm([x_ref], w_ref)
    c = c_ref[...]
    s = s_ref[...]
    lane = lax.broadcasted_iota(jnp.int32, (1, LANES), 1)
    first = (lane % 64) < 32
    for j in range(tn // LANES):
        zj = z[:, LANES * j:LANES * (j + 1)]
        rot = jnp.where(first, pltpu.roll(zj, LANES - 32, 1), pltpu.roll(zj, 32, 1))
        r = zj * c + rot * s
        if scale != 1.0:
            r = r * scale
        for o in o_refs:
            o[:, LANES * j:LANES * (j + 1)] = r.astype(o.dtype)


def _plain_body(x_ref, w_ref, *o_refs):
    z = _mm([x_ref], w_ref)
    for o in o_refs:
        o[...] = z.astype(o.dtype)


def _glu_body(x_ref, wa_ref, wg_ref, ba_ref, bg_ref, o_ref):
    a = _mm([x_ref], wa_ref) + ba_ref[...]
    g = _mm([x_ref], wg_ref) + bg_ref[...]
    o_ref[...] = a * jax.nn.sigmoid(g)


def _res_body(*refs, nparts):
    x_refs = refs[:nparts]
    w_ref, r_ref, o_ref = refs[nparts:]
    o_ref[...] = _mm(x_refs, w_ref) + r_ref[...]


def _proj_rope(x, w, col0, ncols, cs, sn, *, scale, out_dtypes, tm, tn, tab_blocks):
    m, k = x.shape
    cb = col0 // tn
    return pl.pallas_call(
        functools.partial(_rope_body, scale=scale, tn=tn),
        grid=(m // tm, ncols // tn),
        in_specs=[pl.BlockSpec((tm, k), lambda i, j: (i, 0)),
                  pl.BlockSpec((k, tn), lambda i, j: (0, cb + j)),
                  pl.BlockSpec((tm, LANES), lambda i, j: (i % tab_blocks, 0)),
                  pl.BlockSpec((tm, LANES), lambda i, j: (i % tab_blocks, 0))],
        out_specs=[pl.BlockSpec((tm, tn), lambda i, j: (i, j)) for _ in out_dtypes],
        out_shape=[jax.ShapeDtypeStruct((m, ncols), dt) for dt in out_dtypes],
        compiler_params=_cparams("parallel", "arbitrary"),
        name="proj_rope",
    )(x, w, cs, sn)


def _proj_plain(x, w, col0, ncols, *, out_dtypes, tm, tn):
    m, k = x.shape
    cb = col0 // tn
    return pl.pallas_call(
        _plain_body,
        grid=(m // tm, ncols // tn),
        in_specs=[pl.BlockSpec((tm, k), lambda i, j: (i, 0)),
                  pl.BlockSpec((k, tn), lambda i, j: (0, cb + j))],
        out_specs=[pl.BlockSpec((tm, tn), lambda i, j: (i, j)) for _ in out_dtypes],
        out_shape=[jax.ShapeDtypeStruct((m, ncols), dt) for dt in out_dtypes],
        compiler_params=_cparams("parallel", "arbitrary"),
        name="proj_plain",
    )(x, w)


def _proj_glu(x, w, col_a, col_g, ncols, bias, *, tm, tn):
    m, k = x.shape
    ca, cg = col_a // tn, col_g // tn
    nb = ncols // tn
    b2 = bias.reshape(1, 2 * ncols)
    return pl.pallas_call(
        _glu_body,
        grid=(m // tm, nb),
        in_specs=[pl.BlockSpec((tm, k), lambda i, j: (i, 0)),
                  pl.BlockSpec((k, tn), lambda i, j: (0, ca + j)),
                  pl.BlockSpec((k, tn), lambda i, j: (0, cg + j)),
                  pl.BlockSpec((1, tn), lambda i, j: (0, j)),
                  pl.BlockSpec((1, tn), lambda i, j: (0, nb + j))],
        out_specs=pl.BlockSpec((tm, tn), lambda i, j: (i, j)),
        out_shape=jax.ShapeDtypeStruct((m, ncols), F32),
        compiler_params=_cparams("parallel", "arbitrary"),
        name="proj_glu",
    )(x, w, w, b2, b2)


def _proj_res(xs, w, res, *, tm, tn):
    m = xs[0].shape[0]
    k, n = w.shape
    return pl.pallas_call(
        functools.partial(_res_body, nparts=len(xs)),
        grid=(m // tm, n // tn),
        in_specs=[pl.BlockSpec((tm, x.shape[1]), lambda i, j: (i, 0)) for x in xs]
                 + [pl.BlockSpec((k, tn), lambda i, j: (0, j)),
                    pl.BlockSpec((tm, tn), lambda i, j: (i, j))],
        out_specs=pl.BlockSpec((tm, tn), lambda i, j: (i, j)),
        out_shape=jax.ShapeDtypeStruct((m, n), F32),
        compiler_params=_cparams("parallel", "arbitrary"),
        name="proj_res",
    )(*xs, w, res)


def _rope_tables(pos, d):
    inv = ROPE_THETA ** (-jnp.arange(0, d, 2, dtype=F32) / d)
    ang = pos.astype(F32)[:, None] * inv[None, :]
    cos, sin = jnp.cos(ang), jnp.sin(ang)
    reps = LANES // d
    return (jnp.concatenate([cos, cos] * reps, axis=-1),
            jnp.concatenate([-sin, sin] * reps, axis=-1))


def _diff_lambda(lq1, lk1, lq2, lk2, lam_init):
    return (jnp.exp(jnp.sum(lq1[...] * lk1[...], keepdims=True))
            - jnp.exp(jnp.sum(lq2[...] * lk2[...], keepdims=True)) + lam_init)


def _attn_body(lq1, lk1, lq2, lk2, g_ref, q_ref, k_ref, v_ref, km_ref, vm_ref, o_ref,
               m_ref, l_ref, acc_ref, *, tq, n_meta, lam_init, nchunk):
    qi = pl.program_id(2)
    q = q_ref[0]
    lane = lax.broadcasted_iota(jnp.int32, (1, LANES), 1)
    zq = jnp.zeros_like(q)
    qq = jnp.concatenate([jnp.where(lane < 64, q, zq), jnp.where(lane >= 64, q, zq)], axis=0)

    rc = 2 * tq // nchunk
    chunks = [(c, pl.ds(c * rc, rc), qq[c * rc:(c + 1) * rc]) for c in range(nchunk)]

    for _, rows, qc in chunks:
        s = jnp.where(lane < n_meta, _dot_nt(qc, km_ref[...]), -jnp.inf)
        m0 = jnp.max(s, axis=1, keepdims=True)
        p0 = jnp.exp(s - m0)
        m_ref[rows, :] = m0
        l_ref[rows, :] = jnp.sum(p0, axis=1, keepdims=True)
        acc_ref[rows, :] = _dot(p0.astype(BF16), vm_ref[...])

    def update(rows, s, v):
        m_prev = m_ref[rows, :]
        m_new = jnp.maximum(m_prev, jnp.max(s, axis=1, keepdims=True))
        alpha = jnp.exp(m_prev - m_new)
        p = jnp.exp(s - m_new)
        l_ref[rows, :] = alpha * l_ref[rows, :] + jnp.sum(p, axis=1, keepdims=True)
        acc_ref[rows, :] = alpha * acc_ref[rows, :] + _dot(p.astype(BF16), v)
        m_ref[rows, :] = m_new

    def body(kb, carry):
        off = pl.multiple_of(kb * tq, tq)
        k, v = k_ref[0, pl.ds(off, tq), :], v_ref[0, pl.ds(off, tq), :]
        for _, rows, qc in chunks:
            update(rows, _dot_nt(qc, k), v)
        return carry

    lax.fori_loop(0, qi, body, 0)

    off = pl.multiple_of(qi * tq, tq)
    k, v = k_ref[0, pl.ds(off, tq), :], v_ref[0, pl.ds(off, tq), :]
    col = lax.broadcasted_iota(jnp.int32, (rc, tq), 1)
    for c, rows, qc in chunks:
        row = lax.broadcasted_iota(jnp.int32, (rc, tq), 0) + (c * rc) % tq
        update(rows, jnp.where(col <= row, _dot_nt(qc, k), -jnp.inf), v)

    lam = _diff_lambda(lq1, lk1, lq2, lk2, lam_init)
    o = acc_ref[...] / l_ref[...]
    o = o[:tq] - lam * o[tq:]
    o = _rms(o, g_ref[...], NORM_EPS) * (1.0 - lam_init)
    o_ref[0] = o.astype(o_ref.dtype)


def _attn_prompt(q, k, v, km, vm, lams, sub_g, lam_init, tq):
    b, s, w = q.shape
    nh = w // LANES
    vec = lambda a: a.reshape(1, -1)
    small = pl.BlockSpec((1, lams[0].shape[-1]), lambda bi, h, i: (0, 0))
    return pl.pallas_call(
        functools.partial(_attn_body, tq=tq, n_meta=N_META, lam_init=lam_init, nchunk=4),
        grid=(b, nh, s // tq),
        in_specs=[small, small, small, small,
                  pl.BlockSpec((1, LANES), lambda bi, h, i: (0, 0)),
                  pl.BlockSpec((1, tq, LANES), lambda bi, h, i: (bi, i, h)),
                  pl.BlockSpec((1, s, LANES), lambda bi, h, i: (bi, 0, h)),
                  pl.BlockSpec((1, s, LANES), lambda bi, h, i: (bi, 0, h)),
                  pl.BlockSpec((LANES, LANES), lambda bi, h, i: (0, h)),
                  pl.BlockSpec((LANES, LANES), lambda bi, h, i: (0, h))],
        out_specs=pl.BlockSpec((1, tq, LANES), lambda bi, h, i: (bi, i, h)),
        out_shape=jax.ShapeDtypeStruct((b, s, w), BF16),
        scratch_shapes=[pltpu.VMEM((2 * tq, 1), F32), pltpu.VMEM((2 * tq, 1), F32),
                        pltpu.VMEM((2 * tq, LANES), F32)],
        compiler_params=_cparams("parallel", "parallel", "arbitrary"),
        name="attn_prompt",
    )(*[vec(a) for a in lams], vec(sub_g), q, k, v, km, vm)


def _ln_silu(y, g, b):
    mu = jnp.mean(y, axis=-1, keepdims=True)
    d = y - mu
    var = jnp.mean(d * d, axis=-1, keepdims=True)
    z = d * lax.rsqrt(var + LN_EPS) * g + b
    return z * jax.nn.sigmoid(z)


def _conv_body(head_ref, prev_ref, cur_ref, w_ref, b_ref, g_ref, bb_ref, o_ref, ext_ref, y_ref, *, tr, kc, halo):
    i = pl.program_id(1)

    @pl.when(i == 0)
    def _():
        ext_ref[0:halo, :] = head_ref[...]

    @pl.when(i > 0)
    def _():
        ext_ref[0:halo, :] = prev_ref[...]

    ext_ref[halo:halo + tr, :] = cur_ref[...]
    c = cur_ref.shape[1]
    rc, cc = 64, 256
    off0 = halo - (kc - 1)
    for r0 in range(0, tr, rc):
        for c0 in range(0, c, cc):
            acc = jnp.zeros((rc, cc), F32)
            for j in range(kc):
                acc = acc + ext_ref[r0 + off0 + j:r0 + off0 + j + rc, c0:c0 + cc] * w_ref[j:j + 1, c0:c0 + cc]
            y_ref[r0:r0 + rc, c0:c0 + cc] = acc + b_ref[:, c0:c0 + cc]
    o_ref[...] = _ln_silu(y_ref[...], g_ref[...], bb_ref[...]).astype(o_ref.dtype)


def _conv_prompt(u, head, dw_w, dw_b, ln_g, ln_b, nb, tr):
    rows, c = u.shape
    s = rows // nb
    halo = head.shape[0]
    kc = dw_w.shape[0]
    vec = lambda a: a.reshape(1, c)
    cst = lambda bi, i: (0, 0)
    return pl.pallas_call(
        functools.partial(_conv_body, tr=tr, kc=kc, halo=halo),
        grid=(nb, s // tr),
        in_specs=[pl.BlockSpec((halo, c), cst),
                  pl.BlockSpec((halo, c), lambda bi, i: (jnp.maximum((bi * s + i * tr) // halo - 1, 0), 0)),
                  pl.BlockSpec((tr, c), lambda bi, i: (bi * (s // tr) + i, 0)),
                  pl.BlockSpec((kc, c), cst), pl.BlockSpec((1, c), cst),
                  pl.BlockSpec((1, c), cst), pl.BlockSpec((1, c), cst)],
        out_specs=pl.BlockSpec((tr, c), lambda bi, i: (bi * (s // tr) + i, 0)),
        out_shape=jax.ShapeDtypeStruct((rows, c), BF16),
        scratch_shapes=[pltpu.VMEM((halo + tr, c), F32), pltpu.VMEM((tr, c), F32)],
        compiler_params=_cparams("parallel", "arbitrary"),
        name="conv_prompt",
    )(head, u, u, dw_w, vec(dw_b), vec(ln_g), vec(ln_b))


def _sconv_body(st_ref, u_ref, w_ref, b_ref, g_ref, bb_ref, o_ref, *, kc):
    w = w_ref[...]
    y = jnp.sum(st_ref[...] * w[None, :kc - 1, :], axis=1) + u_ref[...] * w[kc - 1:kc, :] + b_ref[...]
    o_ref[...] = _ln_silu(y, g_ref[...], bb_ref[...])


def _conv_sample(state, u, dw_w, dw_b, ln_g, ln_b):
    nb, _, c = state.shape
    vec = lambda a: a.reshape(1, c)
    return pl.pallas_call(
        functools.partial(_sconv_body, kc=dw_w.shape[0]),
        out_shape=jax.ShapeDtypeStruct((nb, c), F32),
        compiler_params=pltpu.CompilerParams(vmem_limit_bytes=VMEM_LIMIT),
        name="conv_sample",
    )(state, u, dw_w, vec(dw_b), vec(ln_g), vec(ln_b))


def _bf(x):
    return x.astype(BF16).astype(F32)


def _sattn_body(pt_ref, lq1, lk1, lq2, lk2, g_ref, e_ref, q_ref, qx_ref, kn_ref, vn_ref, kc_ref, vc_ref, o_ref,
                s_ref, a_ref, acc_ref, anew_ref, pnew_ref, tr_ref, *, nh, npg, page, dv, lam_init):
    p = pl.program_id(1)
    ev = pl.ds(0, nh, stride=2)
    od = pl.ds(1, nh, stride=2)

    @pl.when(p < npg)
    def _():
        s_ref[p] = jnp.sum(_bf(kc_ref[0, 0]) * qx_ref[0], axis=1)

    @pl.when(p == npg - 1)
    def _():
        lam = _diff_lambda(lq1, lk1, lq2, lk2, lam_init)
        sn = jnp.sum(kn_ref[0] * q_ref[0], axis=-1, keepdims=True)
        s = s_ref[...]
        m = jnp.maximum(jnp.max(jnp.max(s, axis=0), axis=-1, keepdims=True), sn)
        e, en = jnp.exp(s - m[None]), jnp.exp(sn - m)
        l = jnp.sum(jnp.sum(e, axis=0), axis=-1, keepdims=True) + en
        s_ref[...] = e / l[None]
        pnew_ref[...] = en / l
        an = pnew_ref[ev, :] - lam * pnew_ref[od, :]
        eye = (lax.broadcasted_iota(jnp.int32, (nh, nh), 0) == lax.broadcasted_iota(jnp.int32, (nh, nh), 1))
        anew_ref[...] = jnp.sum(jnp.where(eye, an, 0.0), axis=0, keepdims=True)
        acc_ref[...] = jnp.zeros(acc_ref.shape, F32)
        tr_ref[...] = jnp.zeros(tr_ref.shape, F32)

        def to_rows(pg, carry):
            tr_ref[0:nh, :] = s_ref[pg, ev, :] - lam * s_ref[pg, od, :]
            a_ref[pl.ds(pl.multiple_of(pg * page, page), page), :] = tr_ref[...].T[:, :nh]
            return carry

        lax.fori_loop(0, npg, to_rows, 0)

    @pl.when(p >= npg)
    def _():
        off = pl.multiple_of((p - npg) * page, page)
        a = a_ref[pl.ds(off, page), :].astype(BF16)
        ae = _dot(a, e_ref[...])
        vd = jnp.concatenate([vc_ref[0, 0, :, h, :] for h in range(nh)], axis=1)
        prod = ae * _bf(vd)
        acc_ref[...] += prod.reshape(page // 8, 8, nh * dv).sum(axis=0)

    @pl.when(p == 2 * npg - 1)
    def _():
        hi, lo = _split(jnp.broadcast_to(anew_ref[...], (16, nh)))
        ane = (_dot(hi, e_ref[...]) + _dot(lo, e_ref[...]))[0:1]
        o = jnp.sum(acc_ref[...], axis=0, keepdims=True) + ane * vn_ref[0]
        for h in range(nh):
            o_ref[0, :, h * dv:(h + 1) * dv] = (_rms(o[:, h * dv:(h + 1) * dv], g_ref[...], NORM_EPS)
                                                * (1.0 - lam_init))


def _attn_sample(q, kn, vn, cache_k, cache_v, page_table, lams, sub_g, lam_init):
    nb, nsub, dq = q.shape
    nh = nsub // 2
    dv = vn.shape[-1] // nh
    npg = page_table.shape[1]
    page = cache_k.shape[2]
    assert page == LANES
    kt = jnp.transpose(cache_k, (0, 1, 3, 4, 2))
    qx = jnp.broadcast_to(_bf(q)[..., None], (nb, nsub, dq, page))
    expand = jnp.asarray(np.kron(np.eye(nh, dtype=np.float32), np.ones((1, dv), np.float32)), BF16)
    vec = lambda a: a.reshape(1, -1)
    small = pl.BlockSpec((1, lams[0].shape[-1]), lambda b, p, pt: (0, 0))
    grid_spec = pltpu.PrefetchScalarGridSpec(
        num_scalar_prefetch=1,
        grid=(nb, 2 * npg),
        in_specs=[small, small, small, small,
                  pl.BlockSpec((1, dv), lambda b, p, pt: (0, 0)),
                  pl.BlockSpec((nh, nh * dv), lambda b, p, pt: (0, 0)),
                  pl.BlockSpec((1, nsub, dq), lambda b, p, pt: (b, 0, 0)),
                  pl.BlockSpec((1, nsub, dq, page), lambda b, p, pt: (b, 0, 0, 0)),
                  pl.BlockSpec((1, nsub, dq), lambda b, p, pt: (b, 0, 0)),
                  pl.BlockSpec((1, 1, nh * dv), lambda b, p, pt: (b, 0, 0)),
                  pl.BlockSpec((1, 1, nsub, dq, page),
                               lambda b, p, pt: (0, pt[b * npg + jnp.minimum(p, npg - 1)], 0, 0, 0)),
                  pl.BlockSpec((1, 1, page, nh, dv),
                               lambda b, p, pt: (0, pt[b * npg + jnp.maximum(p - npg, 0)], 0, 0, 0))],
        out_specs=pl.BlockSpec((1, 1, nh * dv), lambda b, p, pt: (b, 0, 0)),
        scratch_shapes=[pltpu.VMEM((npg, nsub, page), F32), pltpu.VMEM((npg * page, nh), F32),
                        pltpu.VMEM((8, nh * dv), F32), pltpu.VMEM((1, nh), F32), pltpu.VMEM((nsub, 1), F32),
                        pltpu.VMEM((page, page), F32)],
    )
    return pl.pallas_call(
        functools.partial(_sattn_body, nh=nh, npg=npg, page=page, dv=dv, lam_init=lam_init),
        grid_spec=grid_spec,
        out_shape=jax.ShapeDtypeStruct((nb, 1, nh * dv), F32),
        compiler_params=_cparams("parallel", "arbitrary"),
        name="attn_sample",
    )(page_table.reshape(-1), *[vec(a) for a in lams], vec(sub_g), expand, q, qx, kn,
      vn.reshape(nb, 1, nh * dv), kt, cache_v)


def _router_body(x_ref, g_ref, w_ref, b_ref, gate_ref, idx_ref, *, topk):
    tm = x_ref.shape[0]
    h = _rms(x_ref[...], g_ref[...], NORM_EPS)
    logits = _dot(h.astype(BF16), w_ref[...].astype(BF16)) + b_ref[...]
    ne = logits.shape[1]
    eidx = lax.broadcasted_iota(jnp.int32, logits.shape, 1)
    vals, idxs = [], []
    cur = logits
    for _ in range(topk):
        mx = jnp.max(cur, axis=-1, keepdims=True)
        ix = jnp.min(jnp.where(cur == mx, eidx, ne), axis=-1, keepdims=True)
        vals.append(mx)
        idxs.append(ix)
        cur = jnp.where(eidx == ix, -jnp.inf, cur)
    ex = [jnp.exp(v - vals[0]) for v in vals]
    tot = ex[0]
    for e in ex[1:]:
        tot = tot + e
    lane = lax.broadcasted_iota(jnp.int32, (tm, LANES), 1)
    gates = jnp.zeros((tm, LANES), F32)
    ids = jnp.zeros((tm, LANES), jnp.int32)
    for kk in range(topk):
        gates = jnp.where(lane == kk, ex[kk] / tot, gates)
        ids = jnp.where(lane == kk, idxs[kk], ids)
    gate_ref[...] = gates
    idx_ref[...] = ids


def _router(x, g, w, b, tm):
    t, d = x.shape
    ne = w.shape[1]
    return pl.pallas_call(
        functools.partial(_router_body, topk=TOP_K),
        grid=(pl.cdiv(t, tm),),
        in_specs=[pl.BlockSpec((tm, d), lambda i: (i, 0)), pl.BlockSpec((1, d), lambda i: (0, 0)),
                  pl.BlockSpec((d, ne), lambda i: (0, 0)), pl.BlockSpec((1, ne), lambda i: (0, 0))],
        out_specs=[pl.BlockSpec((tm, LANES), lambda i: (i, 0)), pl.BlockSpec((tm, LANES), lambda i: (i, 0))],
        out_shape=[jax.ShapeDtypeStruct((t, LANES), F32), jax.ShapeDtypeStruct((t, LANES), jnp.int32)],
        compiler_params=_cparams("parallel"),
        name="router",
    )(x, g.reshape(1, d), w, b.reshape(1, ne))


def _gather_body(src_ref, nact_ref, x_hbm, g_ref, o_ref, buf, sem, *, tm):
    i = pl.program_id(0)

    @pl.when(i < nact_ref[0])
    def _():
        def issue(r, c):
            tok = src_ref[i * tm + r]
            pltpu.make_async_copy(x_hbm.at[pl.ds(tok, 1)], buf.at[pl.ds(r, 1)], sem).start()
            return c

        lax.fori_loop(0, tm, issue, 0)
        pltpu.make_async_copy(x_hbm.at[pl.ds(0, tm)], buf, sem).wait()
        o_ref[...] = _rms(buf[...], g_ref[...], NORM_EPS).astype(o_ref.dtype)

    @pl.when(i >= nact_ref[0])
    def _():
        o_ref[...] = jnp.zeros(o_ref.shape, o_ref.dtype)


def _moe_gather(x, g, src_tok, n_tiles, ntile_max, tm):
    d = x.shape[1]
    grid_spec = pltpu.PrefetchScalarGridSpec(
        num_scalar_prefetch=2,
        grid=(ntile_max,),
        in_specs=[pl.BlockSpec(memory_space=pl.ANY), pl.BlockSpec((1, d), lambda i, s, n: (0, 0))],
        out_specs=pl.BlockSpec((tm, d), lambda i, s, n: (i, 0)),
        scratch_shapes=[pltpu.VMEM((tm, d), F32), pltpu.SemaphoreType.DMA(())],
    )
    return pl.pallas_call(
        functools.partial(_gather_body, tm=tm),
        grid_spec=grid_spec,
        out_shape=jax.ShapeDtypeStruct((ntile_max * tm, d), BF16),
        compiler_params=_cparams("arbitrary"),
        name="moe_gather",
    )(src_tok, n_tiles.reshape(1), x, g.reshape(1, d))


def _deinterleave_matrix(n):
    p = np.zeros((n, n), np.float32)
    half = n // 2
    p[2 * np.arange(half), np.arange(half)] = 1.0
    p[2 * np.arange(half) + 1, half + np.arange(half)] = 1.0
    return jnp.asarray(p, BF16)


def _moe_mm_body(e_ref, c_ref, t_ref, f_ref, nact_ref, x_ref, w_ref, b_ref, *rest, swiglu, pw):
    if swiglu:
        p_ref, o_ref, wbf = rest
    else:
        o_ref, wbf = rest
    i = pl.program_id(0)

    @pl.when(i < nact_ref[0])
    def _():
        @pl.when(f_ref[i] == 1)
        def _():
            wbf[...] = w_ref[0].astype(BF16)

        z = _dot(x_ref[...], wbf[...]) + b_ref[0]
        if not swiglu:
            o_ref[...] = z
            return
        half = pw // 2
        for j in range(z.shape[1] // pw):
            hi, lo = _split(z[:, pw * j:pw * (j + 1)])
            d = _dot(hi, p_ref[...]) + _dot(lo, p_ref[...])
            gate = jnp.minimum(d[:, :half], SWIGLU_LIMIT)
            up = jnp.clip(d[:, half:], -SWIGLU_LIMIT, SWIGLU_LIMIT)
            act = (up + 1.0) * (gate * jax.nn.sigmoid(SWIGLU_ALPHA * gate))
            o_ref[:, half * j:half * (j + 1)] = act.astype(o_ref.dtype)

    @pl.when(i >= nact_ref[0])
    def _():
        o_ref[...] = jnp.zeros(o_ref.shape, o_ref.dtype)


def _moe_mm(x, w, b, tabs, *, swiglu, tm, tn):
    e_tab, c_tab, t_tab, f_tab, nact = tabs
    ne, k, n = w.shape
    nsteps = e_tab.shape[0]
    rows = x.shape[0]
    n_out = n // 2 if swiglu else n
    tn_out = tn // 2 if swiglu else tn
    pw = 2 * LANES
    in_specs = [pl.BlockSpec((tm, k), lambda i, e, c, t, f, a: (t[i], 0)),
                pl.BlockSpec((1, k, tn), lambda i, e, c, t, f, a: (e[i], 0, c[i])),
                pl.BlockSpec((1, 1, tn), lambda i, e, c, t, f, a: (e[i], 0, c[i]))]
    args = [x, w, b.reshape(ne, 1, n)]
    if swiglu:
        in_specs.append(pl.BlockSpec((pw, pw), lambda i, e, c, t, f, a: (0, 0)))
        args.append(_deinterleave_matrix(pw))
    grid_spec = pltpu.PrefetchScalarGridSpec(
        num_scalar_prefetch=5,
        grid=(nsteps,),
        in_specs=in_specs,
        out_specs=pl.BlockSpec((tm, tn_out), lambda i, e, c, t, f, a: (t[i], c[i])),
        scratch_shapes=[pltpu.VMEM((k, tn), BF16)],
    )
    return pl.pallas_call(
        functools.partial(_moe_mm_body, swiglu=swiglu, pw=pw),
        grid_spec=grid_spec,
        out_shape=jax.ShapeDtypeStruct((rows, n_out), BF16 if swiglu else F32),
        compiler_params=_cparams("arbitrary"),
        name="moe_gate_up" if swiglu else "moe_down",
    )(e_tab, c_tab, t_tab, f_tab, nact, *args)


def _combine_body(dest_ref, x_ref, gate_ref, ys_hbm, fg_ref, o_ref, buf, sem, *, tt, topk):
    i = pl.program_id(0)

    def issue(r, c):
        for kk in range(topk):
            d = dest_ref[(i * tt + r) * topk + kk]
            pltpu.make_async_copy(ys_hbm.at[pl.ds(d, 1)], buf.at[kk, pl.ds(r, 1)], sem).start()
        return c

    lax.fori_loop(0, tt, issue, 0)
    for kk in range(topk):
        pltpu.make_async_copy(ys_hbm.at[pl.ds(0, tt)], buf.at[kk], sem).wait()
    gates = gate_ref[...]
    y = x_ref[...]
    for kk in range(topk):
        y = y + gates[:, kk:kk + 1] * buf[kk]
    o_ref[...] = _rms(y, fg_ref[...], NORM_EPS)


def _moe_combine(x, gates, dest, ys, fg, tt):
    t, d = x.shape
    assert t % tt == 0
    grid_spec = pltpu.PrefetchScalarGridSpec(
        num_scalar_prefetch=1,
        grid=(t // tt,),
        in_specs=[pl.BlockSpec((tt, d), lambda i, s: (i, 0)),
                  pl.BlockSpec((tt, LANES), lambda i, s: (i, 0)),
                  pl.BlockSpec(memory_space=pl.ANY),
                  pl.BlockSpec((1, d), lambda i, s: (0, 0))],
        out_specs=pl.BlockSpec((tt, d), lambda i, s: (i, 0)),
        scratch_shapes=[pltpu.VMEM((TOP_K, tt, d), F32), pltpu.SemaphoreType.DMA(())],
    )
    return pl.pallas_call(
        functools.partial(_combine_body, tt=tt, topk=TOP_K),
        grid_spec=grid_spec,
        out_shape=jax.ShapeDtypeStruct((t, d), F32),
        compiler_params=_cparams("arbitrary"),
        name="moe_combine",
    )(dest, x, gates, ys, fg.reshape(1, d))


def _route_tables(top_i, ne, tm, ntile_max):
    t, topk = top_i.shape
    r = t * topk
    e_flat = top_i.reshape(-1)
    counts = jnp.zeros((ne,), jnp.int32).at[e_flat].add(1)
    ntile = (counts + tm - 1) // tm
    tend = jnp.cumsum(ntile)
    tstart = tend - ntile
    order = jnp.argsort(e_flat, stable=True).astype(jnp.int32)
    e_s = e_flat[order]
    start = jnp.cumsum(counts) - counts
    dest_sorted = tstart[e_s] * tm + jnp.arange(r, dtype=jnp.int32) - start[e_s]
    dest = jnp.zeros((r,), jnp.int32).at[order].set(dest_sorted)
    src_tok = jnp.zeros((ntile_max * tm,), jnp.int32).at[dest_sorted].set(order // topk)
    return dest, src_tok, ntile, tstart, tend[-1]


def _step_tables(ntile, tstart, nchunk, ntile_max):
    per = ntile * nchunk
    send = jnp.cumsum(per)
    sstart = send - per
    nact = send[-1]
    n_tiles = nact // nchunk
    i = jnp.arange(ntile_max * nchunk, dtype=jnp.int32)
    active = i < nact
    ic = jnp.minimum(i, nact - 1)
    e = jnp.searchsorted(send, ic, side="right").astype(jnp.int32)
    local = ic - sstart[e]
    nt = jnp.maximum(ntile[e], 1)
    tail = i - nact
    c = jnp.where(active, local // nt, tail % nchunk)
    tile = jnp.where(active, tstart[e] + local % nt, n_tiles + tail // nchunk)
    first = ((local % nt == 0) & active).astype(jnp.int32)
    return (e, c.astype(jnp.int32), tile.astype(jnp.int32), first, nact.reshape(1).astype(jnp.int32))


def kernel(x_prompt, x_sample, cache_k, cache_v, state_conv, page_table, meta_tokens, norm_mix_g, w_in, b_conv_in,
           lambda_q1, lambda_k1, lambda_q2, lambda_k2, subln_g, dw_w, dw_b, conv_ln_g, conv_ln_b, w_out, norm_ffn_g,
           router_w, router_b, w_gate_up, b_gate_up, w_down, b_down, final_norm_g):
    nb, seq, d = x_prompt.shape
    ns = x_sample.shape[0]
    depth = w_in.shape[0]
    assert depth == 1 and x_sample.shape[1] == 1
    aw = cache_v.shape[-1] * cache_v.shape[-2]
    cw = state_conv.shape[-1]
    nh = cache_v.shape[-2]
    dq = cache_k.shape[-1]
    ne = router_w.shape[-1]
    lam_init = 0.8 - 0.6 * math.exp(-0.3 * 0)
    scale = dq ** -0.5
    lams = (lambda_q1[0], lambda_k1[0], lambda_q2[0], lambda_k2[0])
    w_in0, w_out0 = w_in[0], w_out[0]
    rows = nb * seq
    nsm = N_META + ns

    tm = min(1024, seq)
    tn = 512
    tq = min(512, seq)

    w_in_b = w_in0.astype(BF16)
    xn = _rmsnorm(x_prompt.reshape(rows, d), norm_mix_g[0], BF16, 256)
    cos_p, sin_p = _rope_tables(N_META + jnp.arange(seq), dq)
    (q_b,) = _proj_rope(xn, w_in_b, 0, aw, cos_p, sin_p, scale=scale, out_dtypes=(BF16,),
                        tm=tm, tn=tn, tab_blocks=seq // tm)
    k_f, k_b = _proj_rope(xn, w_in_b, aw, aw, cos_p, sin_p, scale=1.0, out_dtypes=(F32, BF16),
                          tm=tm, tn=tn, tab_blocks=seq // tm)
    v_f, v_b = _proj_plain(xn, w_in_b, 2 * aw, aw, out_dtypes=(F32, BF16), tm=tm, tn=tn)
    u_p = _proj_glu(xn, w_in_b, 3 * aw, 3 * aw + cw, cw, b_conv_in[0], tm=tm, tn=tn)

    x_small = jnp.concatenate([meta_tokens, x_sample[:, 0, :]], axis=0)
    xn_s = _rmsnorm(x_small, norm_mix_g[0], BF16, nsm)
    pos_s = jnp.concatenate([jnp.arange(N_META), jnp.full((ns,), page_table.shape[1] * cache_k.shape[2])])
    cos_s, sin_s = _rope_tables(pos_s, dq)
    (q_s,) = _proj_rope(xn_s, w_in_b, 0, aw, cos_s, sin_s, scale=scale, out_dtypes=(F32,),
                        tm=nsm, tn=tn, tab_blocks=1)
    (k_s,) = _proj_rope(xn_s, w_in_b, aw, aw, cos_s, sin_s, scale=1.0, out_dtypes=(F32,),
                        tm=nsm, tn=tn, tab_blocks=1)
    (v_s,) = _proj_plain(xn_s, w_in_b, 2 * aw, aw, out_dtypes=(F32,), tm=nsm, tn=tn)
    u_s = _proj_glu(xn_s, w_in_b, 3 * aw, 3 * aw + cw, cw, b_conv_in[0], tm=nsm, tn=tn)
    k_meta, v_meta, u_meta = k_s[:N_META], v_s[:N_META], u_s[:N_META]
    q_smp, k_smp, v_smp, u_smp = q_s[N_META:], k_s[N_META:], v_s[N_META:], u_s[N_META:]

    pad = ((0, LANES - N_META), (0, 0))
    km = jnp.pad(k_meta, pad).astype(BF16)
    vm = jnp.pad(v_meta, pad).astype(BF16)
    o_p = _attn_prompt(q_b.reshape(nb, seq, aw), k_b.reshape(nb, seq, aw), v_b.reshape(nb, seq, aw),
                       km, vm, lams, subln_g[0], lam_init, tq)

    halo = 32
    head = jnp.concatenate([jnp.zeros((halo - N_META, cw), F32), u_meta], axis=0)
    yc_p = _conv_prompt(u_p, head, dw_w[0], dw_b[0], conv_ln_g[0], conv_ln_b[0], nb, 128)

    w_out_b = w_out0.astype(BF16)
    x1_p = _proj_res([o_p.reshape(rows, aw), yc_p], w_out_b, x_prompt.reshape(rows, d), tm=tm, tn=tn)

    o_s = _attn_sample(q_smp.reshape(ns, 2 * nh, dq), k_smp.reshape(ns, 2 * nh, dq), v_smp,
                       cache_k, cache_v, page_table, lams, subln_g[0], lam_init)
    yc_s = _conv_sample(state_conv[0], u_smp, dw_w[0], dw_b[0], conv_ln_g[0], conv_ln_b[0])
    x1_s = _proj_res([jnp.concatenate([o_s.reshape(ns, aw), yc_s], axis=1).astype(BF16)], w_out_b,
                     x_sample[:, 0, :], tm=ns, tn=tn)

    x1 = jnp.concatenate([x1_p, x1_s], axis=0)
    t_all = rows + ns
    gates, top_i = _router(x1, norm_ffn_g[0], router_w[0], router_b[0], 256)
    tmm = 512
    ntile_max = -(-(t_all * TOP_K + ne * (tmm - 1)) // tmm)
    dest, src_tok, ntile, tstart, n_tiles = _route_tables(top_i[:, :TOP_K], ne, tmm, ntile_max)
    xs = _moe_gather(x1, norm_ffn_g[0], src_tok, n_tiles, ntile_max, tmm)
    tn_gu = min(1024, w_gate_up.shape[-1])
    tn_dn = min(1024, w_down.shape[-1])
    tabs_gu = _step_tables(ntile, tstart, w_gate_up.shape[-1] // tn_gu, ntile_max)
    tabs_dn = _step_tables(ntile, tstart, w_down.shape[-1] // tn_dn, ntile_max)
    act = _moe_mm(xs, w_gate_up[0], b_gate_up[0], tabs_gu, swiglu=True, tm=tmm, tn=tn_gu)
    ys = _moe_mm(act, w_down[0], b_down[0], tabs_dn, swiglu=False, tm=tmm, tn=tn_dn)
    y = _moe_combine(x1, gates, dest, ys, final_norm_g, 32)

    y_prompt = y[:rows].reshape(nb, seq, d)
    y_sample = y[rows:].reshape(ns, 1, d)
    kq = 2 * nh
    dv = aw // nh
    new_k_prompt = jnp.concatenate([jnp.broadcast_to(k_meta.reshape(1, N_META, kq, dq), (nb, N_META, kq, dq)),
                                    k_f.reshape(nb, seq, kq, dq)], axis=1)[None]
    new_v_prompt = jnp.concatenate([jnp.broadcast_to(v_meta.reshape(1, N_META, nh, dv), (nb, N_META, nh, dv)),
                                    v_f.reshape(nb, seq, nh, dv)], axis=1)[None]
    new_conv_prompt = u_p.reshape(nb, seq, cw)[:, seq - (CONV_K - 1):][None]
    new_k_sample = k_smp.reshape(1, ns, 1, kq, dq)
    new_v_sample = v_smp.reshape(1, ns, 1, nh, dv)
    new_conv_sample = jnp.concatenate([state_conv[0][:, 1:], u_smp[:, None, :]], axis=1)[None]
    return (y_prompt, y_sample, new_k_prompt, new_v_prompt, new_conv_prompt,
            new_k_sample, new_v_sample, new_conv_sample)
```
